```python
import math
import numpy as np
import jax
import jax.numpy as jnp
from jax import lax

D_MODEL = 1024
BATCH = 2
SEQ = 16384
DEPTH = 1
DEC_BATCH = 32
DEC_SEQ = 32
PAST_LEN = 2048

CHUNK = 64
N_MEM = 256
EPS = 1e-6
SSD_WIDTH = D_MODEL // 2
SSD_HEAD_DIM = 64
SSD_HEADS = SSD_WIDTH // SSD_HEAD_DIM
SSD_GROUPS = 2
SSD_STATE = 128
CONV_WIDTH = 4
CONV_DIM = SSD_WIDTH + 2 * SSD_GROUPS * SSD_STATE
DIFF_WIDTH = D_MODEL - SSD_WIDTH
DIFF_HEADS = 4
DIFF_V_DIM = DIFF_WIDTH // DIFF_HEADS
DIFF_QK_DIM = DIFF_V_DIM // 2
ROT_DIM = DIFF_QK_DIM // 4
ROPE_THETA = 500000.0
Q_BLOCK = 128
MEM_HEADS = 4
MEM_HEAD_DIM = D_MODEL // MEM_HEADS
N_KEYS = 128
N_EXPERTS = N_KEYS * N_KEYS
PEER_HEADS = 8
PEER_QUERY_DIM = 256
PEER_HALF = PEER_QUERY_DIM // 2
PEER_TOPK = 16
PEER_TOKEN_BLOCK = 128
IN_SIZES = (SSD_WIDTH, CONV_DIM, SSD_HEADS, DIFF_HEADS * 2 * DIFF_QK_DIM, DIFF_HEADS * 2 * DIFF_QK_DIM, DIFF_HEADS * DIFF_V_DIM)
IN_DIM = sum(IN_SIZES)
IN_SPLITS = tuple(int(s) for s in np.cumsum(IN_SIZES)[:-1])

kernel_name = 'hybrid_ssd_diffattn_peer_streaming_step'


def rmsnorm(x, g):
    xf = x.astype(jnp.float32)
    r = xf * lax.rsqrt(jnp.mean(xf * xf, axis=-1, keepdims=True) + EPS)
    return (r * g.astype(jnp.float32)).astype(x.dtype)


def rope_partial(t, pos):
    half = ROT_DIM // 2
    inv = 1.0 / (ROPE_THETA ** (jnp.arange(half, dtype=jnp.float32) / half))
    ang = pos[:, None] * inv[None, :]
    cos = jnp.cos(ang)[None, :, None, None, :].astype(t.dtype)
    sin = jnp.sin(ang)[None, :, None, None, :].astype(t.dtype)
    t1 = t[..., :half]
    t2 = t[..., half:ROT_DIM]
    return jnp.concatenate([t1 * cos - t2 * sin, t2 * cos + t1 * sin, t[..., ROT_DIM:]], axis=-1)


def causal_conv(xbc, prev, w, b):
    L = xbc.shape[1]
    xpad = jnp.concatenate([prev.astype(xbc.dtype), xbc], axis=1)
    out = b + sum(xpad[:, j:j + L] * w[j] for j in range(CONV_WIDTH))
    return jax.nn.silu(out), xpad[:, -(CONV_WIDTH - 1):]


def ssd_scan(x, dt, a, b_in, c_in, d_skip, h0):
    f32 = jnp.float32
    bsz, L = x.shape[:2]
    q = min(CHUNK, L)
    nc = L // q
    rep = SSD_HEADS // SSD_GROUPS
    xf = x.astype(f32)
    bh = jnp.repeat(b_in.astype(f32), rep, axis=2)
    ch = jnp.repeat(c_in.astype(f32), rep, axis=2)
    xdt = xf * dt[..., None]
    da = dt * a
    blk = lambda t: t.reshape((bsz, nc, q) + t.shape[2:])
    xdt, bh, ch, da = blk(xdt), blk(bh), blk(ch), blk(da)
    a_cum = jnp.cumsum(da, axis=2)
    seg = a_cum[:, :, :, None, :] - a_cum[:, :, None, :, :]
    causal = jnp.tril(jnp.ones((q, q), dtype=bool))[None, None, :, :, None]
    decay = jnp.exp(jnp.where(causal, seg, -jnp.inf))
    cb = jnp.einsum('bcihn,bcjhn->bcijh', ch, bh)
    y_diag = jnp.einsum('bcijh,bcjhp->bcihp', cb * decay, xdt)
    to_end = jnp.exp(a_cum[:, :, -1:, :] - a_cum)
    states = jnp.einsum('bcjhn,bcjh,bcjhp->bchpn', bh, to_end, xdt)
    chunk_decay = jnp.exp(a_cum[:, :, -1, :])

    def step(h, inp):
        s, dec = inp
        return h * dec[:, :, None, None] + s, h

    h_last, h_in = lax.scan(step, h0.astype(f32), (jnp.moveaxis(states, 1, 0), jnp.moveaxis(chunk_decay, 1, 0)))
    h_in = jnp.moveaxis(h_in, 0, 1)
    y_off = jnp.einsum('bcihn,bchpn,bcih->bcihp', ch, h_in, jnp.exp(a_cum))
    y = (y_diag + y_off).reshape(bsz, L, SSD_HEADS, SSD_HEAD_DIM) + d_skip.astype(f32)[:, None] * xf
    return y.astype(x.dtype), h_last


def diff_weights(s, lam):
    p = jax.nn.softmax(s, axis=-1)
    return p[:, :, 0] - lam * p[:, :, 1]


def diff_attn_blockwise(q, k, v, lam):
    bsz, S = q.shape[:2]
    nblk = S // Q_BLOCK
    scale = 1.0 / math.sqrt(DIFF_QK_DIM)
    qb = jnp.swapaxes(q.reshape((bsz, nblk, Q_BLOCK) + q.shape[2:]), 0, 1)
    key_chunk = jnp.arange(S) // CHUNK

    def one(args):
        qi, bi = args
        q_chunk = (bi * Q_BLOCK + jnp.arange(Q_BLOCK)) // CHUNK
        mask = key_chunk[None, :] <= q_chunk[:, None]
        s = jnp.einsum('bqhcd,bkhcd->bhcqk', qi, k).astype(jnp.float32) * scale
        s = jnp.where(mask, s, -jnp.inf)
        a = diff_weights(s, lam)
        return jnp.einsum('bhqk,bkhd->bqhd', a.astype(v.dtype), v)

    out = lax.map(one, (qb, jnp.arange(nblk)))
    return jnp.swapaxes(out, 0, 1).reshape(bsz, S, DIFF_HEADS, DIFF_V_DIM)


def diff_attn_dense(q, k, v, lam):
    s = jnp.einsum('bqhcd,bkhcd->bhcqk', q, k).astype(jnp.float32) / math.sqrt(DIFF_QK_DIM)
    a = diff_weights(s, lam)
    return jnp.einsum('bhqk,bkhd->bqhd', a.astype(v.dtype), v)


def memory_kv(mem, g, w_k, w_v):
    m = rmsnorm(mem, g)
    shp = mem.shape[:2] + (MEM_HEADS, MEM_HEAD_DIM)
    return (m @ w_k).reshape(shp), (m @ w_v).reshape(shp)


def peer_ffn(h, w_pq, sub_keys, u_tab, v_tab):
    bsz, L, d = h.shape
    n = bsz * L
    nblk = -(-n // PEER_TOKEN_BLOCK)
    t = jnp.pad(h.reshape(n, d), ((0, nblk * PEER_TOKEN_BLOCK - n), (0, 0)))

    def one(tb):
        qy = (tb @ w_pq).reshape(tb.shape[0], PEER_HEADS, 2, PEER_HALF)
        s = jnp.einsum('thcd,hcnd->thcn', qy, sub_keys).astype(jnp.float32)
        s1, i1 = lax.top_k(s[:, :, 0], PEER_TOPK)
        s2, i2 = lax.top_k(s[:, :, 1], PEER_TOPK)
        comb = (s1[..., :, None] + s2[..., None, :]).reshape(tb.shape[0], PEER_HEADS, PEER_TOPK * PEER_TOPK)
        sv, si = lax.top_k(comb, PEER_TOPK)
        e = jnp.take_along_axis(i1, si // PEER_TOPK, axis=-1) * N_KEYS + jnp.take_along_axis(i2, si % PEER_TOPK, axis=-1)
        g = jax.nn.softmax(sv, axis=-1)
        act = jax.nn.gelu(jnp.einsum('thkd,td->thk', u_tab[e], tb).astype(jnp.float32)) * g
        return jnp.einsum('thk,thkd->td', act.astype(tb.dtype), v_tab[e])

    out = lax.map(one, t.reshape(nblk, PEER_TOKEN_BLOCK, d))
    return out.reshape(-1, d)[:n].reshape(bsz, L, d)


def trunk_layer(x, pos_offset, k_past, v_past, h0, conv_prev, mem_k, mem_v, p, lambda_init):
    f32 = jnp.float32
    bsz, L, _ = x.shape
    h = rmsnorm(x, p['g_mix'])
    z, xbc, dt_raw, q, k, v = jnp.split(h @ p['w_in'], IN_SPLITS, axis=-1)
    xbc, conv_new = causal_conv(xbc, conv_prev, p['conv_w'], p['conv_b'])
    xs, b_in, c_in = jnp.split(xbc, (SSD_WIDTH, SSD_WIDTH + SSD_GROUPS * SSD_STATE), axis=-1)
    dt = jax.nn.softplus(dt_raw.astype(f32) + p['dt_bias'].astype(f32))
    a = -jnp.exp(p['a_log'].astype(f32))
    y_ssd, h_last = ssd_scan(xs.reshape(bsz, L, SSD_HEADS, SSD_HEAD_DIM), dt, a,
                             b_in.reshape(bsz, L, SSD_GROUPS, SSD_STATE),
                             c_in.reshape(bsz, L, SSD_GROUPS, SSD_STATE), p['d_skip'], h0)
    y_ssd = rmsnorm(y_ssd.reshape(bsz, L, SSD_WIDTH) * jax.nn.silu(z), p['g_ssd'])
    pos = jnp.arange(L, dtype=f32) + pos_offset
    q = rope_partial(q.reshape(bsz, L, DIFF_HEADS, 2, DIFF_QK_DIM), pos)
    k = rope_partial(k.reshape(bsz, L, DIFF_HEADS, 2, DIFF_QK_DIM), pos)
    v = v.reshape(bsz, L, DIFF_HEADS, DIFF_V_DIM)
    lam = (jnp.exp(jnp.sum(p['lam_q1'].astype(f32) * p['lam_k1'].astype(f32)))
           - jnp.exp(jnp.sum(p['lam_q2'].astype(f32) * p['lam_k2'].astype(f32))) + lambda_init)
    if k_past is None:
        o = diff_attn_blockwise(q, k, v, lam)
    else:
        n_past = k_past.shape[1]
        k_all = jnp.concatenate([k_past.reshape(bsz, n_past, DIFF_HEADS, 2, DIFF_QK_DIM).astype(k.dtype), k], axis=1)
        v_all = jnp.concatenate([v_past.astype(v.dtype), v], axis=1)
        o = diff_attn_dense(q, k_all, v_all, lam)
    o = rmsnorm(o, p['g_subln']) * (1.0 - lambda_init)
    mixed = jnp.concatenate([y_ssd, o.reshape(bsz, L, DIFF_WIDTH).astype(y_ssd.dtype)], axis=-1)
    x = x + mixed @ p['w_out']
    qm = (rmsnorm(x, p['g_mem_q']) @ p['w_mq']).reshape(bsz, L, MEM_HEADS, MEM_HEAD_DIM)
    sm = jnp.einsum('blhd,bmhd->bhlm', qm, mem_k.astype(qm.dtype)).astype(f32) / math.sqrt(MEM_HEAD_DIM)
    pm = jax.nn.softmax(sm, axis=-1).astype(x.dtype)
    om = jnp.einsum('bhlm,bmhd->blhd', pm, mem_v.astype(x.dtype)).reshape(bsz, L, D_MODEL)
    x = x + om @ p['w_mo']
    x = x + peer_ffn(rmsnorm(x, p['g_ffn']), p['w_pq'], p['peer_keys'], p['peer_u'], p['peer_v'])
    k_rows = k.reshape(bsz, L, DIFF_HEADS, 2 * DIFF_QK_DIM)
    return x, k_rows, v, h_last.astype(x.dtype), conv_new


def setup_inputs(seed: int = 0) -> dict:
    key = jax.random.key(seed)
    ks = iter(jax.random.split(key, 48))
    f32 = jnp.float32
    nrm = lambda shape, scale: jax.random.normal(next(ks), shape, f32) * scale
    gain = lambda shape: 1.0 + 0.02 * jax.random.normal(next(ks), shape, f32)
    dsc = D_MODEL ** -0.5
    x_prompt = nrm((BATCH, SEQ, D_MODEL), 1.0)
    x_sample = nrm((DEC_BATCH, DEC_SEQ, D_MODEL), 1.0)
    cache_attn_k = nrm((DEPTH, DEC_BATCH, PAST_LEN, DIFF_HEADS, 2 * DIFF_QK_DIM), 1.0)
    cache_attn_v = nrm((DEPTH, DEC_BATCH, PAST_LEN, DIFF_HEADS, DIFF_V_DIM), 1.0)
    cache_mem_k = nrm((DEPTH, DEC_BATCH, N_MEM, MEM_HEADS, MEM_HEAD_DIM), 1.0)
    cache_mem_v = nrm((DEPTH, DEC_BATCH, N_MEM, MEM_HEADS, MEM_HEAD_DIM), 1.0)
    state_ssm = nrm((DEPTH, DEC_BATCH, SSD_HEADS, SSD_HEAD_DIM, SSD_STATE), 0.1)
    state_conv = nrm((DEPTH, DEC_BATCH, CONV_WIDTH - 1, CONV_DIM), 1.0)
    mem_prompt = nrm((BATCH, N_MEM, D_MODEL), 1.0)
    g_mix = gain((DEPTH, D_MODEL))
    w_in = nrm((DEPTH, D_MODEL, IN_DIM), dsc)
    conv_w = nrm((DEPTH, CONV_WIDTH, CONV_DIM), CONV_WIDTH ** -0.5)
    conv_b = nrm((DEPTH, CONV_DIM), 0.01)
    dt0 = jnp.exp(jax.random.uniform(next(ks), (DEPTH, SSD_HEADS), f32, math.log(1e-3), math.log(1e-1)))
    dt_bias = dt0 + jnp.log(-jnp.expm1(-dt0))
    a_log = jnp.log(jax.random.uniform(next(ks), (DEPTH, SSD_HEADS), f32, 1.0, 16.0))
    d_skip = gain((DEPTH, SSD_HEADS))
    g_ssd = gain((DEPTH, SSD_WIDTH))
    lam_q1 = nrm((DEPTH, DIFF_QK_DIM), 0.1)
    lam_k1 = nrm((DEPTH, DIFF_QK_DIM), 0.1)
    lam_q2 = nrm((DEPTH, DIFF_QK_DIM), 0.1)
    lam_k2 = nrm((DEPTH, DIFF_QK_DIM), 0.1)
    g_subln = gain((DEPTH, DIFF_V_DIM))
    w_out = nrm((DEPTH, D_MODEL, D_MODEL), dsc)
    g_mem_q = gain((DEPTH, D_MODEL))
    g_mem_kv = gain((DEPTH, D_MODEL))
    w_mq = nrm((DEPTH, D_MODEL, D_MODEL), dsc)
    w_mk = nrm((DEPTH, D_MODEL, D_MODEL), dsc)
    w_mv = nrm((DEPTH, D_MODEL, D_MODEL), dsc)
    w_mo = nrm((DEPTH, D_MODEL, D_MODEL), dsc)
    g_ffn = gain((DEPTH, D_MODEL))
    w_pq = nrm((DEPTH, D_MODEL, PEER_HEADS * PEER_QUERY_DIM), dsc)
    peer_keys = nrm((DEPTH, PEER_HEADS, 2, N_KEYS, PEER_HALF), PEER_HALF ** -0.5)
    peer_u = nrm((DEPTH, N_EXPERTS, D_MODEL), dsc)
    peer_v = nrm((DEPTH, N_EXPERTS, D_MODEL), (PEER_HEADS * PEER_TOPK) ** -0.5)
    g_final = gain((D_MODEL,))
    return {'x_prompt': x_prompt, 'x_sample': x_sample, 'cache_attn_k': cache_attn_k, 'cache_attn_v': cache_attn_v,
            'cache_mem_k': cache_mem_k, 'cache_mem_v': cache_mem_v, 'state_ssm': state_ssm, 'state_conv': state_conv,
            'mem_prompt': mem_prompt, 'g_mix': g_mix, 'w_in': w_in, 'conv_w': conv_w, 'conv_b': conv_b,
            'dt_bias': dt_bias, 'a_log': a_log, 'd_skip': d_skip, 'g_ssd': g_ssd, 'lam_q1': lam_q1, 'lam_k1': lam_k1,
            'lam_q2': lam_q2, 'lam_k2': lam_k2, 'g_subln': g_subln, 'w_out': w_out, 'g_mem_q': g_mem_q,
            'g_mem_kv': g_mem_kv, 'w_mq': w_mq, 'w_mk': w_mk, 'w_mv': w_mv, 'w_mo': w_mo, 'g_ffn': g_ffn,
            'w_pq': w_pq, 'peer_keys': peer_keys, 'peer_u': peer_u, 'peer_v': peer_v, 'g_final': g_final}


def reference(x_prompt, x_sample, cache_attn_k, cache_attn_v, cache_mem_k, cache_mem_v, state_ssm, state_conv,
              mem_prompt, g_mix, w_in, conv_w, conv_b, dt_bias, a_log, d_skip, g_ssd, lam_q1, lam_k1, lam_q2, lam_k2,
              g_subln, w_out, g_mem_q, g_mem_kv, w_mq, w_mk, w_mv, w_mo, g_ffn, w_pq, peer_keys, peer_u, peer_v,
              g_final):
    xp = x_prompt
    xs = x_sample
    bp = xp.shape[0]
    past_len = cache_attn_k.shape[2]
    kp_l, vp_l, hp_l, cp_l, mkp_l, mvp_l = [], [], [], [], [], []
    ks_l, vs_l, hs_l, cs_l = [], [], [], []
    for l in range(DEPTH):
        p = {'g_mix': g_mix[l], 'w_in': w_in[l], 'conv_w': conv_w[l], 'conv_b': conv_b[l], 'dt_bias': dt_bias[l],
             'a_log': a_log[l], 'd_skip': d_skip[l], 'g_ssd': g_ssd[l], 'lam_q1': lam_q1[l], 'lam_k1': lam_k1[l],
             'lam_q2': lam_q2[l], 'lam_k2': lam_k2[l], 'g_subln': g_subln[l], 'w_out': w_out[l],
             'g_mem_q': g_mem_q[l], 'w_mq': w_mq[l], 'w_mo': w_mo[l], 'g_ffn': g_ffn[l], 'w_pq': w_pq[l],
             'peer_keys': peer_keys[l], 'peer_u': peer_u[l], 'peer_v': peer_v[l]}
        lambda_init = 0.8 - 0.6 * math.exp(-0.3 * l)
        mk_p, mv_p = memory_kv(mem_prompt, g_mem_kv[l], w_mk[l], w_mv[l])
        h0_p = jnp.zeros((bp, SSD_HEADS, SSD_HEAD_DIM, SSD_STATE), jnp.float32)
        conv0_p = jnp.zeros((bp, CONV_WIDTH - 1, CONV_DIM), xp.dtype)
        xp, kp, vp, hp, cp = trunk_layer(xp, 0, None, None, h0_p, conv0_p, mk_p, mv_p, p, lambda_init)
        xs, ksn, vsn, hsn, csn = trunk_layer(xs, past_len, cache_attn_k[l], cache_attn_v[l], state_ssm[l],
                                             state_conv[l], cache_mem_k[l], cache_mem_v[l], p, lambda_init)
        kp_l.append(kp); vp_l.append(vp); hp_l.append(hp); cp_l.append(cp); mkp_l.append(mk_p); mvp_l.append(mv_p)
        ks_l.append(ksn); vs_l.append(vsn); hs_l.append(hsn); cs_l.append(csn)
    y_prompt = rmsnorm(xp, g_final)
    y_sample = rmsnorm(xs, g_final)
    return (y_prompt, y_sample, jnp.stack(kp_l), jnp.stack(vp_l), jnp.stack(hp_l), jnp.stack(cp_l),
            jnp.stack(mkp_l), jnp.stack(mvp_l), jnp.stack(ks_l), jnp.stack(vs_l), jnp.stack(hs_l), jnp.stack(cs_l))
```

```python
import functools
import math

import jax
import jax.numpy as jnp
from jax import lax
from jax.experimental import pallas as pl
from jax.experimental.pallas import tpu as pltpu

F32 = jnp.float32
BF16 = jnp.bfloat16
I32 = jnp.int32
EPS = 1e-6
LANES = 128
CHUNK = 64
ROT_HALF = 8
ROPE_THETA = 500000.0
CONV_WIDTH = 4
PEER_TOPK = 16
VMEM_LIMIT = 56 * 1024 * 1024
HIGHEST = lax.Precision.HIGHEST
NT_DIMS = (((1,), (1,)), ((), ()))


def _cparams(*sem):
    return pltpu.CompilerParams(dimension_semantics=sem, vmem_limit_bytes=VMEM_LIMIT)


def _rms(x, g):
    return x * lax.rsqrt(jnp.mean(x * x, axis=-1, keepdims=True) + EPS) * g


def _mm(a, b):
    return jnp.dot(a, b, preferred_element_type=F32)


def _mm_nt(a, b):
    return lax.dot_general(a, b, NT_DIMS, preferred_element_type=F32)


def _full(shape):
    return pl.BlockSpec(shape, lambda *_: (0,) * len(shape))


def _in_proj_kernel(x_ref, g_ref, wz_ref, wx_ref, wdt_ref, wq_ref, wk_ref, wv_ref, c_ref, s1_ref, s2_ref,
                    z_ref, xbc_ref, dt_ref, q_ref, k_ref, v_ref, kb_ref, vb_ref, *, q_scale):
    h = _rms(x_ref[...], g_ref[...]).astype(BF16)
    z_ref[...] = _mm(h, wz_ref[...])
    xbc_ref[...] = _mm(h, wx_ref[...])
    dt_ref[...] = _mm(h, wdt_ref[...])
    c, s1, s2 = c_ref[...], s1_ref[...], s2_ref[...]

    def rope(t):
        parts = []
        for gi in range(t.shape[1] // LANES):
            tg = t[:, gi * LANES:(gi + 1) * LANES]
            parts.append(tg * c + pltpu.roll(tg, ROT_HALF, 1) * s1 + pltpu.roll(tg, LANES - ROT_HALF, 1) * s2)
        return jnp.concatenate(parts, axis=1)

    q_ref[...] = (rope(_mm(h, wq_ref[...])) * q_scale).astype(BF16)
    k = rope(_mm(h, wk_ref[...]))
    k_ref[...] = k
    kb_ref[...] = k.astype(BF16)
    v = _mm(h, wv_ref[...])
    v_ref[...] = v
    vb_ref[...] = v.astype(BF16)


def _rope_tables(length, offset, qk_dim):
    inv = 1.0 / (ROPE_THETA ** (jnp.arange(ROT_HALF, dtype=F32) / ROT_HALF))
    pos = jnp.arange(length, dtype=F32) + offset
    ang = pos[:, None] * inv[None, :]
    cos, sin = jnp.cos(ang), jnp.sin(ang)
    rest = qk_dim - 2 * ROT_HALF
    one, zero = jnp.ones((length, rest), F32), jnp.zeros((length, rest), F32)
    z8 = jnp.zeros((length, ROT_HALF), F32)
    rep = LANES // qk_dim
    c = jnp.tile(jnp.concatenate([cos, cos, one], axis=1), (1, rep))
    s1 = jnp.tile(jnp.concatenate([z8, sin, zero], axis=1), (1, rep))
    s2 = jnp.tile(jnp.concatenate([-sin, z8, zero], axis=1), (1, rep))
    return c, s1, s2


def _in_proj(x, g, ws, tables, tm, rows_per_table_cycle, q_scale):
    n, d = x.shape
    wz, wx, wdt, wq, wk, wv = ws
    nt = rows_per_table_cycle // tm
    row = lambda w: pl.BlockSpec((tm, w), lambda i: (i, 0))
    tab = pl.BlockSpec((tm, LANES), lambda i: (i % nt, 0))
    widths = [wz.shape[1], wx.shape[1], wdt.shape[1], wq.shape[1], wk.shape[1], wv.shape[1]]
    out_shape = [jax.ShapeDtypeStruct((n, w), F32) for w in widths[:3]]
    out_shape += [jax.ShapeDtypeStruct((n, widths[3]), BF16), jax.ShapeDtypeStruct((n, widths[4]), F32),
                  jax.ShapeDtypeStruct((n, widths[5]), F32), jax.ShapeDtypeStruct((n, widths[4]), BF16),
                  jax.ShapeDtypeStruct((n, widths[5]), BF16)]
    out_specs = [row(w) for w in widths[:3]] + [row(widths[3]), row(widths[4]), row(widths[5]), row(widths[4]),
                                                 row(widths[5])]
    return pl.pallas_call(
        functools.partial(_in_proj_kernel, q_scale=q_scale),
        grid=(n // tm,),
        in_specs=[row(d), _full((1, d))] + [_full(w.shape) for w in ws] + [tab, tab, tab],
        out_specs=out_specs, out_shape=out_shape,
        compiler_params=_cparams("parallel"),
    )(x, g, *ws, *tables)


def _norm_mm2_kernel(x_ref, g_ref, wa_ref, wb_ref, a_ref, b_ref):
    h = _rms(x_ref[...], g_ref[...]).astype(BF16)
    a_ref[...] = _mm(h, wa_ref[...])
    b_ref[...] = _mm(h, wb_ref[...])


def _norm_mm2(x, g, wa, wb, tm):
    n, d = x.shape
    row = lambda w: pl.BlockSpec((tm, w), lambda i: (i, 0))
    return pl.pallas_call(
        _norm_mm2_kernel, grid=(n // tm,),
        in_specs=[row(d), _full((1, d)), _full(wa.shape), _full(wb.shape)],
        out_specs=[row(wa.shape[1]), row(wb.shape[1])],
        out_shape=[jax.ShapeDtypeStruct((n, wa.shape[1]), F32), jax.ShapeDtypeStruct((n, wb.shape[1]), F32)],
        compiler_params=_cparams("parallel"),
    )(x, g, wa, wb)


def _pad_rows_to_lanes(x):
    q = x.shape[0]
    if q == LANES:
        return x
    return jnp.concatenate([x, jnp.zeros((LANES - q, x.shape[1]), x.dtype)], axis=0)


def _ssd_kernel(xbc_ref, dt_ref, z_ref, h0_ref, cprev_ref, cw_ref, cb_ref, dtb_ref, alog_ref, dskip_ref, gssd_ref,
                expand_ref, y_ref, hl_ref, cn_ref, xp_ref, xc_ref, s_ref, *, q, tb, n_heads, head_dim, n_state,
                n_groups):
    t = pl.program_id(1)
    width = n_heads * head_dim
    gw = width // n_groups
    hpg = n_heads // n_groups
    halo = 8

    @pl.when(t == 0)
    def _():
        s_ref[...] = h0_ref[0]
        xp_ref[0:halo, :] = cprev_ref[0]

    xp_ref[halo:halo + tb, :] = xbc_ref[...]
    cw = cw_ref[...]
    conv = cb_ref[...]
    for j in range(CONV_WIDTH):
        off = halo - (CONV_WIDTH - 1) + j
        conv = conv + cw[j:j + 1, :] * xp_ref[off:off + tb, :]
    xc_ref[...] = conv * jax.nn.sigmoid(conv)
    cn_ref[0] = xp_ref[halo + tb - (CONV_WIDTH - 1):halo + tb, :]
    xp_ref[0:halo, :] = xp_ref[tb:tb + halo, :]

    a_neg = -jnp.exp(alog_ref[...])
    ri = lax.broadcasted_iota(I32, (q, q), 0)
    ci = lax.broadcasted_iota(I32, (q, q), 1)
    tril = ci <= ri
    tril_f = tril.astype(F32)
    lane_g = lax.broadcasted_iota(I32, (1, gw), 1)

    def chunk(c, carry):
        r0 = pl.multiple_of(c * q, q)
        xc = xc_ref[pl.ds(r0, q), :]
        xs = xc[:, :width]
        bm = xc[:, width:width + n_groups * n_state]
        cm = xc[:, width + n_groups * n_state:]
        dtr = dt_ref[pl.ds(r0, q), :] + dtb_ref[...]
        dt = jnp.maximum(dtr, 0.0) + jnp.log1p(jnp.exp(-jnp.abs(dtr)))
        da = dt * a_neg
        a_cum = jnp.dot(tril_f, da, precision=HIGHEST, preferred_element_type=F32)
        a_last = a_cum[q - 1:q, :]
        stk = jnp.concatenate([dt, jnp.exp(a_last - a_cum), jnp.exp(a_cum)], axis=0)
        ex = jnp.dot(stk, expand_ref[...], precision=HIGHEST, preferred_element_type=F32)
        dt_x, te_x, ec_x = ex[:q], ex[q:2 * q], ex[2 * q:]
        cd_x = ec_x[q - 1:q, :]
        xdt = xs * dt_x
        wgt = xdt * te_x
        a_cum_t = _pad_rows_to_lanes(a_cum).T
        ys = []
        for g in range(n_groups):
            bg = bm[:, g * n_state:(g + 1) * n_state]
            cg = cm[:, g * n_state:(g + 1) * n_state].astype(BF16)
            bg_t = _pad_rows_to_lanes(bg).T[:, :q].astype(BF16)
            cb = _mm_nt(cg, bg.astype(BF16))
            s_in = s_ref[g]
            gs = slice(g * gw, (g + 1) * gw)
            y_g = _mm(cg, s_in.astype(BF16)) * ec_x[:, gs] + dskip_ref[:, gs] * xs[:, gs]
            xdt_g = xdt[:, gs]
            for hl in range(hpg):
                h = g * hpg + hl
                seg = a_cum[:, h:h + 1] - a_cum_t[h:h + 1, :q]
                m_h = (cb * jnp.exp(jnp.where(tril, seg, -jnp.inf))).astype(BF16)
                in_head = (lane_g >= hl * head_dim) & (lane_g < (hl + 1) * head_dim)
                y_g = y_g + _mm(m_h, jnp.where(in_head, xdt_g, 0.0).astype(BF16))
            s_ref[g] = s_in * cd_x[:, gs] + _mm(bg_t, wgt[:, gs].astype(BF16))
            ys.append(y_g)
        y = jnp.concatenate(ys, axis=1)
        zc = z_ref[pl.ds(r0, q), :]
        y_ref[pl.ds(r0, q), :] = _rms(y * (zc * jax.nn.sigmoid(zc)), gssd_ref[...])
        return carry

    lax.fori_loop(0, tb // q, chunk, 0)

    @pl.when(t == pl.num_programs(1) - 1)
    def _():
        hl_ref[0] = s_ref[...]


def _ssd(xbc, dt, z, h0_t, cprev8, cw, cb, dtb, alog, dskip_x, gssd, expand, *, bsz, length, q, tb, n_heads,
         head_dim, n_state, n_groups):
    nt = length // tb
    width = n_heads * head_dim
    cdim = xbc.shape[1]
    row = lambda w: pl.BlockSpec((tb, w), lambda b, t: (b * nt + t, 0))
    gw = width // n_groups
    kern = functools.partial(_ssd_kernel, q=q, tb=tb, n_heads=n_heads, head_dim=head_dim, n_state=n_state,
                             n_groups=n_groups)
    return pl.pallas_call(
        kern, grid=(bsz, nt),
        in_specs=[row(cdim), row(LANES), row(width),
                  pl.BlockSpec((1, n_groups, n_state, gw), lambda b, t: (b, 0, 0, 0)),
                  pl.BlockSpec((1, 8, cdim), lambda b, t: (b, 0, 0)),
                  _full(cw.shape), _full(cb.shape), _full(dtb.shape), _full(alog.shape), _full(dskip_x.shape),
                  _full(gssd.shape), _full(expand.shape)],
        out_specs=[row(width),
                   pl.BlockSpec((1, n_groups, n_state, gw), lambda b, t: (b, 0, 0, 0)),
                   pl.BlockSpec((1, CONV_WIDTH - 1, cdim), lambda b, t: (b, 0, 0))],
        out_shape=[jax.ShapeDtypeStruct((bsz * length, width), F32),
                   jax.ShapeDtypeStruct((bsz, n_groups, n_state, gw), F32),
                   jax.ShapeDtypeStruct((bsz, CONV_WIDTH - 1, cdim), F32)],
        scratch_shapes=[pltpu.VMEM((tb + 8, cdim), F32), pltpu.VMEM((tb, cdim), F32),
                        pltpu.VMEM((n_groups, n_state, gw), F32)],
        compiler_params=_cparams("parallel", "arbitrary"),
    )(xbc, dt, z, h0_t, cprev8, cw, cb, dtb, alog, dskip_x, gssd, expand)


def _lambda(lq1_ref, lk1_ref, lq2_ref, lk2_ref, lam_init):
    e1 = jnp.exp(jnp.sum(lq1_ref[...] * lk1_ref[...], axis=1, keepdims=True))
    e2 = jnp.exp(jnp.sum(lq2_ref[...] * lk2_ref[...], axis=1, keepdims=True))
    return e1 - e2 + lam_init


def _split_components(qh, qk_dim):
    lane = lax.broadcasted_iota(I32, qh.shape, 1)
    zero = jnp.zeros_like(qh)
    return jnp.where(lane < qk_dim, qh, zero), jnp.where(lane >= qk_dim, qh, zero)


def _attn_prompt_kernel(q_ref, k_ref, vt_ref, lq1_ref, lk1_ref, lq2_ref, lk2_ref, gs_ref, o_ref,
                        acc1, acc2, m1, l1, m2, l2, *, tq, qk_dim, lam_init):
    qi = pl.program_id(2)
    qq1, qq2 = _split_components(q_ref[...], qk_dim)
    for acc, m, l in ((acc1, m1, l1), (acc2, m2, l2)):
        acc[...] = jnp.zeros_like(acc)
        m[...] = jnp.full_like(m, -jnp.inf)
        l[...] = jnp.zeros_like(l)
    kr = lax.broadcasted_iota(I32, (tq, tq), 0) // CHUNK
    qc = lax.broadcasted_iota(I32, (tq, tq), 1) // CHUNK
    visible = kr <= qc

    def step(j, masked):
        kk = k_ref[pl.ds(pl.multiple_of(j * tq, tq), tq), :]
        vt = vt_ref[j]
        for qq, acc, m, l in ((qq1, acc1, m1, l1), (qq2, acc2, m2, l2)):
            s = _mm_nt(kk, qq)
            if masked:
                s = jnp.where(visible, s, -jnp.inf)
            m_old = m[...]
            m_new = jnp.maximum(m_old, jnp.max(s, axis=0, keepdims=True))
            alpha = jnp.exp(m_old - m_new)
            p = jnp.exp(s - m_new)
            l[...] = alpha * l[...] + jnp.sum(p, axis=0, keepdims=True)
            acc[...] = alpha * acc[...] + _mm(vt, p.astype(BF16))
            m[...] = m_new

    def body(j, carry):
        step(j, False)
        return carry

    lax.fori_loop(0, qi, body, 0)
    step(qi, True)

    lam = _lambda(lq1_ref, lk1_ref, lq2_ref, lk2_ref, lam_init)
    o = acc1[...] * (1.0 / l1[...]) - lam * (acc2[...] * (1.0 / l2[...]))
    r = o * lax.rsqrt(jnp.mean(o * o, axis=0, keepdims=True) + EPS) * gs_ref[...]
    o_ref[...] = (r * (1.0 - lam_init)).T


def _attn_prompt(qb, kb, vt, lams, gs_col, *, tq, n_heads, qk_dim, lam_init):
    bsz, s, _ = qb.shape
    vd = vt.shape[3]
    nblk = s // tq
    kern = functools.partial(_attn_prompt_kernel, tq=tq, qk_dim=qk_dim, lam_init=lam_init)
    return pl.pallas_call(
        kern, grid=(bsz, n_heads, nblk),
        in_specs=[pl.BlockSpec((None, tq, 2 * qk_dim), lambda b, h, i: (b, i, h)),
                  pl.BlockSpec((None, s, 2 * qk_dim), lambda b, h, i: (b, 0, h)),
                  pl.BlockSpec((None, None, nblk, vd, tq), lambda b, h, i: (b, h, 0, 0, 0))]
                 + [_full(a.shape) for a in lams] + [_full(gs_col.shape)],
        out_specs=pl.BlockSpec((None, tq, vd), lambda b, h, i: (b, i, h)),
        out_shape=jax.ShapeDtypeStruct((bsz, s, n_heads * vd), F32),
        scratch_shapes=[pltpu.VMEM((vd, tq), F32), pltpu.VMEM((vd, tq), F32)] + [pltpu.VMEM((1, tq), F32)] * 4,
        compiler_params=_cparams("parallel", "parallel", "arbitrary"),
    )(qb, kb, vt, *lams, gs_col)


def _attn_sample_kernel(q_ref, kn_ref, vn_ref, kc_ref, vc_ref, lq1_ref, lk1_ref, lq2_ref, lk2_ref, gs_ref, o_ref,
                        *, n_heads, qk_dim, lam_init):
    lam = _lambda(lq1_ref, lk1_ref, lq2_ref, lk2_ref, lam_init)
    vd = 2 * qk_dim
    for h in range(n_heads):
        sl = slice(h * vd, (h + 1) * vd)
        qq = _split_components(q_ref[:, sl], qk_dim)
        kp = kc_ref[:, sl].astype(BF16)
        vp = vc_ref[:, sl].astype(BF16)
        kn = kn_ref[:, sl]
        vn = vn_ref[:, sl]
        outs = []
        for qc in qq:
            sp = _mm_nt(qc, kp)
            sn = _mm_nt(qc, kn)
            m = jnp.maximum(jnp.max(sp, axis=1, keepdims=True), jnp.max(sn, axis=1, keepdims=True))
            pp = jnp.exp(sp - m)
            pn = jnp.exp(sn - m)
            l = jnp.sum(pp, axis=1, keepdims=True) + jnp.sum(pn, axis=1, keepdims=True)
            outs.append((_mm(pp.astype(BF16), vp) + _mm(pn.astype(BF16), vn)) * (1.0 / l))
        o = outs[0] - lam * outs[1]
        o_ref[:, sl] = _rms(o, gs_ref[...]) * (1.0 - lam_init)


def _attn_sample(qb, kb, vb, kc, vc, lams, gs_row, *, n_heads, qk_dim, lam_init):
    bsz, length, w = qb.shape
    past = kc.shape[1]
    new = pl.BlockSpec((None, length, w), lambda b: (b, 0, 0))
    old = pl.BlockSpec((None, past, w), lambda b: (b, 0, 0))
    kern = functools.partial(_attn_sample_kernel, n_heads=n_heads, qk_dim=qk_dim, lam_init=lam_init)
    return pl.pallas_call(
        kern, grid=(bsz,),
        in_specs=[new, new, new, old, old] + [_full(a.shape) for a in lams] + [_full(gs_row.shape)],
        out_specs=new, out_shape=jax.ShapeDtypeStruct((bsz, length, w), F32),
        compiler_params=_cparams("parallel"),
    )(qb, kb, vb, kc, vc, *lams, gs_row)


def _mid_kernel(x_ref, ys_ref, oa_ref, woa_ref, wob_ref, gq_ref, wmq_ref, mk_ref, mv_ref, wmo_ref, x2_ref,
                *, n_heads, scale):
    x1 = x_ref[...] + _mm(ys_ref[...].astype(BF16), woa_ref[...]) + _mm(oa_ref[...].astype(BF16), wob_ref[...])
    qm = _mm(_rms(x1, gq_ref[...]).astype(BF16), wmq_ref[...])
    hd = qm.shape[1] // n_heads
    oms = []
    for h in range(n_heads):
        sl = slice(h * hd, (h + 1) * hd)
        s = _mm_nt(qm[:, sl].astype(BF16), mk_ref[:, sl].astype(BF16)) * scale
        p = jnp.exp(s - jnp.max(s, axis=1, keepdims=True))
        p = p * (1.0 / jnp.sum(p, axis=1, keepdims=True))
        oms.append(_mm(p.astype(BF16), mv_ref[:, sl].astype(BF16)))
    om = jnp.concatenate(oms, axis=1)
    x2_ref[...] = x1 + _mm(om.astype(BF16), wmo_ref[...])


def _mid(x, ys, oa, woa, wob, gq, wmq, mk, mv, wmo, *, tm, rows_per_batch, n_heads):
    n, d = x.shape
    tiles_per_batch = rows_per_batch // tm
    n_mem = mk.shape[1]
    row = lambda w: pl.BlockSpec((tm, w), lambda i: (i, 0))
    mem = pl.BlockSpec((None, n_mem, d), lambda i: (i // tiles_per_batch, 0, 0))
    kern = functools.partial(_mid_kernel, n_heads=n_heads, scale=1.0 / math.sqrt(d // n_heads))
    return pl.pallas_call(
        kern, grid=(n // tm,),
        in_specs=[row(d), row(ys.shape[1]), row(oa.shape[1]), _full(woa.shape), _full(wob.shape), _full(gq.shape),
                  _full(wmq.shape), mem, mem, _full(wmo.shape)],
        out_specs=row(d), out_shape=jax.ShapeDtypeStruct((n, d), F32),
        compiler_params=_cparams("parallel"),
    )(x, ys, oa, woa, wob, gq, wmq, mk, mv, wmo)


def _topk_rows(s, k):
    n, t = s.shape
    rows = lax.broadcasted_iota(I32, (n, t), 0)
    slot = lax.broadcasted_iota(I32, (k, t), 0)
    vals = jnp.zeros((k, t), F32)
    idxs = jnp.zeros((k, t), I32)
    for i in range(k):
        m = jnp.max(s, axis=0, keepdims=True)
        idx = jnp.min(jnp.where(s == m, rows, n), axis=0, keepdims=True)
        s = jnp.where(rows == idx, -jnp.inf, s)
        vals = jnp.where(slot == i, m, vals)
        idxs = jnp.where(slot == i, idx, idxs)
    return vals, idxs


def _pair_candidates(k):
    rows = [0 * k + j for j in range(k)]
    rows += [1 * k + j for j in range(8)]
    for i in range(2, 8):
        rows += [(i * k + j) if (i + 1) * (j + 1) <= k else -1 for j in range(8)]
    rows += [i * k for i in range(8, k)]
    return rows


def _peer_score_kernel(x2_ref, gf_ref, wpqt_ref, keys_ref, fid_ref, eid_ref, gate_ref, qt_ref, e_s, g_s,
                       *, n_heads, n_keys, topk):
    hf = _rms(x2_ref[...], gf_ref[...]).astype(BF16)
    qt_ref[...] = _mm_nt(wpqt_ref[...], hf).astype(BF16)
    half = wpqt_ref.shape[0] // (2 * n_heads)
    fid = fid_ref[...]
    tm = fid.shape[1]
    slot = lax.broadcasted_iota(I32, (topk, tm), 0)

    def head(h, carry):
        tops = []
        for c in range(2):
            r0 = pl.multiple_of((2 * h + c) * half, half)
            tops.append(_topk_rows(_mm(keys_ref[h, c], qt_ref[pl.ds(r0, half), :]), topk))
        (s1, i1), (s2, i2) = tops
        blocks_s = [s1[0:1] + s2, s1[1:2] + s2[0:8]]
        blocks_e = [i1[0:1] * n_keys + i2, i1[1:2] * n_keys + i2[0:8]]
        for i in range(2, 8):
            blocks_s.append(s1[i:i + 1] + s2[0:8])
            blocks_e.append(i1[i:i + 1] * n_keys + i2[0:8])
        blocks_s.append(s1[8:topk] + s2[0:1])
        blocks_e.append(i1[8:topk] * n_keys + i2[0:1])
        cand = jnp.where(fid >= 0, jnp.concatenate(blocks_s, axis=0), -jnp.inf)
        eid = jnp.concatenate(blocks_e, axis=0)
        sv = jnp.zeros((topk, tm), F32)
        ev = jnp.zeros((topk, tm), I32)
        for i in range(topk):
            m = jnp.max(cand, axis=0, keepdims=True)
            f = jnp.min(jnp.where(cand == m, fid, topk * topk), axis=0, keepdims=True)
            sel = fid == f
            e = jnp.max(jnp.where(sel, eid, -1), axis=0, keepdims=True)
            cand = jnp.where(sel, -jnp.inf, cand)
            sv = jnp.where(slot == i, m, sv)
            ev = jnp.where(slot == i, e, ev)
        p = jnp.exp(sv - sv[0:1])
        r0 = pl.multiple_of(h * topk, topk)
        e_s[pl.ds(r0, topk), :] = ev
        g_s[pl.ds(r0, topk), :] = p * (1.0 / jnp.sum(p, axis=0, keepdims=True))
        return carry

    lax.fori_loop(0, n_heads, head, 0)
    eid_ref[...] = e_s[...].T
    gate_ref[...] = g_s[...].T


def _peer_score(x2, gf, wpqt, keys, *, tm):
    n, d = x2.shape
    n_heads, _, n_keys, _ = keys.shape
    picks = n_heads * PEER_TOPK
    fid = jnp.tile(jnp.asarray(_pair_candidates(PEER_TOPK), I32)[:, None], (1, tm))
    kern = functools.partial(_peer_score_kernel, n_heads=n_heads, n_keys=n_keys, topk=PEER_TOPK)
    row = lambda w: pl.BlockSpec((tm, w), lambda i: (i, 0))
    return pl.pallas_call(
        kern, grid=(n // tm,),
        in_specs=[row(d), _full(gf.shape), _full(wpqt.shape), _full(keys.shape), _full(fid.shape)],
        out_specs=[row(picks), row(picks)],
        out_shape=[jax.ShapeDtypeStruct((n, picks), I32), jax.ShapeDtypeStruct((n, picks), F32)],
        scratch_shapes=[pltpu.VMEM((wpqt.shape[0], tm), BF16), pltpu.VMEM((picks, tm), I32),
                        pltpu.VMEM((picks, tm), F32)],
        compiler_params=_cparams("parallel"),
    )(x2, gf, wpqt, keys, fid)


def _gelu_tanh(x):
    return 0.5 * x * (1.0 + jnp.tanh(math.sqrt(2.0 / math.pi) * (x + 0.044715 * (x * x * x))))


def _peer_gather_kernel(eid_ref, gate_ref, x2_ref, gf_ref, gfin_ref, tab_ref, y_ref, buf, sem, *, tt, picks):
    d = x2_ref.shape[1]

    def issue(i, carry):
        e = eid_ref[i // picks, i % picks]
        pltpu.make_async_copy(tab_ref.at[pl.ds(e, 1), :], buf.at[pl.ds(i, 1), :], sem).start()
        return carry

    lax.fori_loop(0, tt * picks, issue, 0, unroll=8)
    pltpu.make_async_copy(tab_ref.at[pl.ds(0, tt * picks), :], buf, sem).wait()
    for t in range(tt):
        x2 = x2_ref[t:t + 1, :]
        hf = _rms(x2, gf_ref[...])
        rows = buf[t * picks:(t + 1) * picks, :]
        a = jnp.sum(rows[:, :d] * hf, axis=1, keepdims=True)
        act = _gelu_tanh(a) * gate_ref[:, t:t + 1]
        o = jnp.sum(rows[:, d:] * act, axis=0, keepdims=True)
        y_ref[t:t + 1, :] = _rms(x2 + o, gfin_ref[...])


def _peer_gather(eid, gate_t, x2, gf, gfin, tab, *, tt):
    n, d = x2.shape
    picks = eid.shape[1]
    kern = functools.partial(_peer_gather_kernel, tt=tt, picks=picks)
    return pl.pallas_call(
        kern, grid=(n // tt,),
        in_specs=[pl.BlockSpec((tt, picks), lambda i: (i, 0), memory_space=pltpu.SMEM),
                  pl.BlockSpec((None, picks, tt), lambda i: (i, 0, 0)),
                  pl.BlockSpec((tt, d), lambda i: (i, 0)), _full(gf.shape), _full(gfin.shape),
                  pl.BlockSpec(memory_space=pl.ANY)],
        out_specs=pl.BlockSpec((tt, d), lambda i: (i, 0)),
        out_shape=jax.ShapeDtypeStruct((n, d), F32),
        scratch_shapes=[pltpu.VMEM((tt * picks, tab.shape[1]), F32), pltpu.SemaphoreType.DMA(())],
        compiler_params=_cparams("arbitrary"),
    )(eid, gate_t, x2, gf, gfin, tab)


def _layer(x, pos_offset, k_past, v_past, h0, conv_prev, mem_k, mem_v, p, lam_init, dims, tiles):
    bsz, length, d = x.shape
    n = bsz * length
    nh, hd, ns, ng = dims["ssd_heads"], dims["ssd_head_dim"], dims["ssd_state"], dims["ssd_groups"]
    ah, qk = dims["diff_heads"], dims["qk_dim"]
    vd = 2 * qk
    x2d = x.reshape(n, d)

    tables = _rope_tables(length, pos_offset, qk)
    tm = tiles["tm"]
    if tm > length:
        tables = tuple(jnp.tile(t, (tm // length, 1)) for t in tables)
        cycle = tm
    else:
        cycle = length
    z, xbc, dt, qb, k, v, kb, vb = _in_proj(x2d, p["g_mix"], p["w_in_parts"], tables, tm, cycle, 1.0 / math.sqrt(qk))

    gw = nh * hd // ng
    h0_t = h0.reshape(bsz, ng, nh // ng, hd, ns).transpose(0, 1, 4, 2, 3).reshape(bsz, ng, ns, gw)
    cprev8 = jnp.pad(conv_prev, ((0, 0), (8 - (CONV_WIDTH - 1), 0), (0, 0)))
    q_len = min(CHUNK, length)
    y_ssd, h_last_t, conv_new = _ssd(xbc, dt, z, h0_t, cprev8, p["conv_w"], p["conv_b"], p["dt_bias_pad"],
                                     p["a_log_pad"], p["d_skip_x"], p["g_ssd"], p["expand"], bsz=bsz, length=length,
                                     q=q_len, tb=tiles["ssd_tb"], n_heads=nh, head_dim=hd, n_state=ns, n_groups=ng)
    h_last = h_last_t.reshape(bsz, ng, ns, nh // ng, hd).transpose(0, 1, 3, 4, 2).reshape(bsz, nh, hd, ns)

    lams = (p["lam_q1"], p["lam_k1"], p["lam_q2"], p["lam_k2"])
    if k_past is None:
        tq = tiles["attn_tq"]
        vt = vb.reshape(bsz, length // tq, tq, ah, vd).transpose(0, 3, 1, 4, 2)
        o = _attn_prompt(qb.reshape(bsz, length, ah * vd), kb.reshape(bsz, length, ah * vd), vt, lams,
                         p["g_subln"].reshape(vd, 1), tq=tq, n_heads=ah, qk_dim=qk, lam_init=lam_init)
    else:
        past = k_past.shape[1]
        o = _attn_sample(qb.reshape(bsz, length, ah * vd), kb.reshape(bsz, length, ah * vd),
                         vb.reshape(bsz, length, ah * vd), k_past.reshape(bsz, past, ah * vd),
                         v_past.reshape(bsz, past, ah * vd), lams, p["g_subln"].reshape(1, vd), n_heads=ah,
                         qk_dim=qk, lam_init=lam_init)

    n_mem = mem_k.shape[1]
    x2 = _mid(x2d, y_ssd, o.reshape(n, ah * vd), p["w_out_a"], p["w_out_b"], p["g_mem_q"], p["w_mq"],
              mem_k.reshape(bsz, n_mem, d), mem_v.reshape(bsz, n_mem, d), p["w_mo"], tm=tiles["mid_tm"],
              rows_per_batch=length, n_heads=dims["mem_heads"])

    eid, gate = _peer_score(x2, p["g_ffn"], p["w_pq_t"], p["peer_keys"], tm=tiles["peer_tm"])
    tt = tiles["gather_tt"]
    picks = eid.shape[1]
    gate_t = gate.reshape(n // tt, tt, picks).transpose(0, 2, 1)
    y = _peer_gather(eid, gate_t, x2, p["g_ffn"], p["g_final"], p["peer_tab"], tt=tt)
    return (y.reshape(bsz, length, d), k.reshape(bsz, length, ah, vd), v.reshape(bsz, length, ah, vd), h_last,
            conv_new)


def kernel(x_prompt, x_sample, cache_attn_k, cache_attn_v, cache_mem_k, cache_mem_v, state_ssm, state_conv, mem_prompt, g_mix, w_in, conv_w, conv_b, dt_bias, a_log, d_skip, g_ssd, lam_q1, lam_k1, lam_q2, lam_k2, g_subln, w_out, g_mem_q, g_mem_kv, w_mq, w_mk, w_mv, w_mo, g_ffn, w_pq, peer_keys, peer_u, peer_v, g_final):
    depth = w_in.shape[0]
    assert depth == 1, "the final norm is fused into the last layer's PEER kernel; one layer supported"
    bp, seq, d = x_prompt.shape
    _, _, nh, hd, ns = state_ssm.shape
    cdim = state_conv.shape[-1]
    width = nh * hd
    ng = (cdim - width) // (2 * ns)
    ah = cache_attn_k.shape[3]
    qk = lam_q1.shape[-1]
    dims = dict(ssd_heads=nh, ssd_head_dim=hd, ssd_state=ns, ssd_groups=ng, diff_heads=ah, qk_dim=qk,
                mem_heads=cache_mem_k.shape[3])
    sizes = (width, cdim, nh, ah * 2 * qk, ah * 2 * qk, ah * 2 * qk)
    splits = [sum(sizes[:i + 1]) for i in range(len(sizes) - 1)]
    n_mem = mem_prompt.shape[1]

    l = 0
    lam_init = 0.8 - 0.6 * math.exp(-0.3 * l)
    wz, wx, wdt, wq, wk, wv = jnp.split(w_in[l].astype(BF16), splits, axis=1)
    wdt = jnp.pad(wdt, ((0, 0), (0, LANES - nh)))
    pad_h = lambda a: jnp.pad(a.reshape(1, nh), ((0, 0), (0, LANES - nh)))
    expand = (jnp.arange(LANES)[:, None] == (jnp.arange(width) // hd)[None, :]).astype(F32)
    p = dict(
        g_mix=g_mix[l].reshape(1, d), w_in_parts=(wz, wx, wdt, wq, wk, wv),
        conv_w=conv_w[l], conv_b=conv_b[l].reshape(1, cdim), dt_bias_pad=pad_h(dt_bias[l]), a_log_pad=pad_h(a_log[l]),
        d_skip_x=jnp.repeat(d_skip[l], hd).reshape(1, width), g_ssd=g_ssd[l].reshape(1, width), expand=expand,
        lam_q1=lam_q1[l].reshape(1, qk), lam_k1=lam_k1[l].reshape(1, qk), lam_q2=lam_q2[l].reshape(1, qk),
        lam_k2=lam_k2[l].reshape(1, qk), g_subln=g_subln[l],
        w_out_a=w_out[l, :width].astype(BF16), w_out_b=w_out[l, width:].astype(BF16),
        g_mem_q=g_mem_q[l].reshape(1, d), w_mq=w_mq[l].astype(BF16), w_mo=w_mo[l].astype(BF16),
        g_ffn=g_ffn[l].reshape(1, d), w_pq_t=w_pq[l].T.astype(BF16), peer_keys=peer_keys[l].astype(BF16),
        peer_tab=jnp.concatenate([peer_u[l], peer_v[l]], axis=1), g_final=g_final.reshape(1, d),
    )

    mk_p, mv_p = _norm_mm2(mem_prompt.reshape(bp * n_mem, d), g_mem_kv[l].reshape(1, d), w_mk[l].astype(BF16),
                           w_mv[l].astype(BF16), tm=n_mem)
    mem_shape = (bp, n_mem) + cache_mem_k.shape[3:]
    mk_p, mv_p = mk_p.reshape(mem_shape), mv_p.reshape(mem_shape)

    tiles_p = dict(tm=min(512, seq), ssd_tb=min(512, seq), attn_tq=min(512, seq), mid_tm=min(512, seq),
                   peer_tm=256, gather_tt=8)
    h0_p = jnp.zeros((bp, nh, hd, ns), F32)
    conv0_p = jnp.zeros((bp, CONV_WIDTH - 1, cdim), F32)
    yp, kp, vp, hp, cp = _layer(x_prompt, 0, None, None, h0_p, conv0_p, mk_p, mv_p, p, lam_init, dims, tiles_p)

    bs, dseq, _ = x_sample.shape
    past = cache_attn_k.shape[2]
    ns_rows = bs * dseq
    tiles_s = dict(tm=min(512, ns_rows), ssd_tb=dseq, mid_tm=dseq, peer_tm=min(256, ns_rows), gather_tt=8)
    ys, ksn, vsn, hsn, csn = _layer(x_sample, past, cache_attn_k[l], cache_attn_v[l], state_ssm[l], state_conv[l],
                                    cache_mem_k[l], cache_mem_v[l], p, lam_init, dims, tiles_s)

    st = lambda a: a[None]
    return (yp, ys, st(kp), st(vp), st(hp), st(cp), st(mk_p), st(mv_p), st(ksn), st(vsn), st(hsn), st(csn))
```

```python
import functools
import math

import jax
import jax.numpy as jnp
from jax import lax
from jax.experimental import pallas as pl
from jax.experimental.pallas import tpu as pltpu

F32 = jnp.float32
BF16 = jnp.bfloat16
I32 = jnp.int32
EPS = 1e-6
LANES = 128
CHUNK = 64
ROT_HALF = 8
ROPE_THETA = 500000.0
CONV_WIDTH = 4
PEER_TOPK = 16
VMEM_LIMIT = 56 * 1024 * 1024
HIGHEST = lax.Precision.HIGHEST
NT_DIMS = (((1,), (1,)), ((), ()))


def _cparams(*sem):
    return pltpu.CompilerParams(dimension_semantics=sem, vmem_limit_bytes=VMEM_LIMIT)


def _rms(x, g):
    return x * lax.rsqrt(jnp.mean(x * x, axis=-1, keepdims=True) + EPS) * g


def _mm(a, b):
    return jnp.dot(a, b, preferred_element_type=F32)


def _mm_nt(a, b):
    return lax.dot_general(a, b, NT_DIMS, preferred_element_type=F32)


def _full(shape):
    return pl.BlockSpec(shape, lambda *_: (0,) * len(shape))


def _in_proj_kernel(x_ref, g_ref, wz_ref, wx_ref, wdt_ref, wq_ref, wk_ref, wv_ref, c_ref, s1_ref, s2_ref,
                    z_ref, xbc_ref, dt_ref, q_ref, k_ref, v_ref, kb_ref, vb_ref, *, q_scale):
    h = _rms(x_ref[...], g_ref[...]).astype(BF16)
    z_ref[...] = _mm(h, wz_ref[...])
    xbc_ref[...] = _mm(h, wx_ref[...])
    dt_ref[...] = _mm(h, wdt_ref[...])
    c, s1, s2 = c_ref[...], s1_ref[...], s2_ref[...]

    def rope(t):
        parts = []
        for gi in range(t.shape[1] // LANES):
            tg = t[:, gi * LANES:(gi + 1) * LANES]
            parts.append(tg * c + pltpu.roll(tg, ROT_HALF, 1) * s1 + pltpu.roll(tg, LANES - ROT_HALF, 1) * s2)
        return jnp.concatenate(parts, axis=1)

    q_ref[...] = (rope(_mm(h, wq_ref[...])) * q_scale).astype(BF16)
    k = rope(_mm(h, wk_ref[...]))
    k_ref[...] = k
    kb_ref[...] = k.astype(BF16)
    v = _mm(h, wv_ref[...])
    v_ref[...] = v
    vb_ref[...] = v.astype(BF16)


def _rope_tables(length, offset, qk_dim):
    inv = 1.0 / (ROPE_THETA ** (jnp.arange(ROT_HALF, dtype=F32) / ROT_HALF))
    pos = jnp.arange(length, dtype=F32) + offset
    ang = pos[:, None] * inv[None, :]
    cos, sin = jnp.cos(ang), jnp.sin(ang)
    rest = qk_dim - 2 * ROT_HALF
    one, zero = jnp.ones((length, rest), F32), jnp.zeros((length, rest), F32)
    z8 = jnp.zeros((length, ROT_HALF), F32)
    rep = LANES // qk_dim
    c = jnp.tile(jnp.concatenate([cos, cos, one], axis=1), (1, rep))
    s1 = jnp.tile(jnp.concatenate([z8, sin, zero], axis=1), (1, rep))
    s2 = jnp.tile(jnp.concatenate([-sin, z8, zero], axis=1), (1, rep))
    return c, s1, s2


def _in_proj(x, g, ws, tables, tm, rows_per_table_cycle, q_scale):
    n, d = x.shape
    wz, wx, wdt, wq, wk, wv = ws
    nt = rows_per_table_cycle // tm
    row = lambda w: pl.BlockSpec((tm, w), lambda i: (i, 0))
    tab = pl.BlockSpec((tm, LANES), lambda i: (i % nt, 0))
    widths = [wz.shape[1], wx.shape[1], wdt.shape[1], wq.shape[1], wk.shape[1], wv.shape[1]]
    out_shape = [jax.ShapeDtypeStruct((n, w), F32) for w in widths[:3]]
    out_shape += [jax.ShapeDtypeStruct((n, widths[3]), BF16), jax.ShapeDtypeStruct((n, widths[4]), F32),
                  jax.ShapeDtypeStruct((n, widths[5]), F32), jax.ShapeDtypeStruct((n, widths[4]), BF16),
                  jax.ShapeDtypeStruct((n, widths[5]), BF16)]
    out_specs = [row(w) for w in widths[:3]] + [row(widths[3]), row(widths[4]), row(widths[5]), row(widths[4]),
                                                 row(widths[5])]
    return pl.pallas_call(
        functools.partial(_in_proj_kernel, q_scale=q_scale),
        grid=(n // tm,),
        in_specs=[row(d), _full((1, d))] + [_full(w.shape) for w in ws] + [tab, tab, tab],
        out_specs=out_specs, out_shape=out_shape,
        compiler_params=_cparams("parallel"),
    )(x, g, *ws, *tables)


def _norm_mm2_kernel(x_ref, g_ref, wa_ref, wb_ref, a_ref, b_ref):
    h = _rms(x_ref[...], g_ref[...]).astype(BF16)
    a_ref[...] = _mm(h, wa_ref[...])
    b_ref[...] = _mm(h, wb_ref[...])


def _norm_mm2(x, g, wa, wb, tm):
    n, d = x.shape
    row = lambda w: pl.BlockSpec((tm, w), lambda i: (i, 0))
    return pl.pallas_call(
        _norm_mm2_kernel, grid=(n // tm,),
        in_specs=[row(d), _full((1, d)), _full(wa.shape), _full(wb.shape)],
        out_specs=[row(wa.shape[1]), row(wb.shape[1])],
        out_shape=[jax.ShapeDtypeStruct((n, wa.shape[1]), F32), jax.ShapeDtypeStruct((n, wb.shape[1]), F32)],
        compiler_params=_cparams("parallel"),
    )(x, g, wa, wb)


def _pad_rows_to_lanes(x):
    q = x.shape[0]
    if q == LANES:
        return x
    return jnp.concatenate([x, jnp.zeros((LANES - q, x.shape[1]), x.dtype)], axis=0)


def _ssd_kernel(xbc_ref, dt_ref, z_ref, h0_ref, cprev_ref, cw_ref, cb_ref, dtb_ref, alog_ref, dskip_ref, gssd_ref,
                expand_ref, y_ref, hl_ref, cn_ref, xp_ref, xc_ref, s_ref, *, q, tb, n_heads, head_dim, n_state,
                n_groups):
    t = pl.program_id(1)
    width = n_heads * head_dim
    gw = width // n_groups
    hpg = n_heads // n_groups
    halo = 8

    @pl.when(t == 0)
    def _():
        s_ref[...] = h0_ref[0]
        xp_ref[0:halo, :] = cprev_ref[0]

    xp_ref[halo:halo + tb, :] = xbc_ref[...]
    cw = cw_ref[...]
    conv = cb_ref[...]
    for j in range(CONV_WIDTH):
        off = halo - (CONV_WIDTH - 1) + j
        conv = conv + cw[j:j + 1, :] * xp_ref[off:off + tb, :]
    xc_ref[...] = conv * jax.nn.sigmoid(conv)
    cn_ref[0] = xp_ref[halo + tb - (CONV_WIDTH - 1):halo + tb, :]
    xp_ref[0:halo, :] = xp_ref[tb:tb + halo, :]

    a_neg = -jnp.exp(alog_ref[...])
    ri = lax.broadcasted_iota(I32, (q, q), 0)
    ci = lax.broadcasted_iota(I32, (q, q), 1)
    tril = ci <= ri
    tril_f = tril.astype(F32)
    lane_g = lax.broadcasted_iota(I32, (1, gw), 1)

    def chunk(c, carry):
        r0 = pl.multiple_of(c * q, q)
        xc = xc_ref[pl.ds(r0, q), :]
        xs = xc[:, :width]
        bm = xc[:, width:width + n_groups * n_state]
        cm = xc[:, width + n_groups * n_state:]
        dtr = dt_ref[pl.ds(r0, q), :] + dtb_ref[...]
        dt = jnp.maximum(dtr, 0.0) + jnp.log1p(jnp.exp(-jnp.abs(dtr)))
        da = dt * a_neg
        a_cum = jnp.dot(tril_f, da, precision=HIGHEST, preferred_element_type=F32)
        a_last = a_cum[q - 1:q, :]
        stk = jnp.concatenate([dt, jnp.exp(a_last - a_cum), jnp.exp(a_cum)], axis=0)
        ex = jnp.dot(stk, expand_ref[...], precision=HIGHEST, preferred_element_type=F32)
        dt_x, te_x, ec_x = ex[:q], ex[q:2 * q], ex[2 * q:]
        cd_x = ec_x[q - 1:q, :]
        xdt = xs * dt_x
        wgt = xdt * te_x
        a_cum_t = _pad_rows_to_lanes(a_cum).T
        ys = []
        for g in range(n_groups):
            bg = bm[:, g * n_state:(g + 1) * n_state]
            cg = cm[:, g * n_state:(g + 1) * n_state].astype(BF16)
            bg_t = _pad_rows_to_lanes(bg).T[:, :q].astype(BF16)
            cb = _mm_nt(cg, bg.astype(BF16))
            s_in = s_ref[g]
            gs = slice(g * gw, (g + 1) * gw)
            y_g = _mm(cg, s_in.astype(BF16)) * ec_x[:, gs] + dskip_ref[:, gs] * xs[:, gs]
            xdt_g = xdt[:, gs]
            for hl in range(hpg):
                h = g * hpg + hl
                seg = a_cum[:, h:h + 1] - a_cum_t[h:h + 1, :q]
                m_h = (cb * jnp.exp(jnp.where(tril, seg, -jnp.inf))).astype(BF16)
                in_head = (lane_g >= hl * head_dim) & (lane_g < (hl + 1) * head_dim)
                y_g = y_g + _mm(m_h, jnp.where(in_head, xdt_g, 0.0).astype(BF16))
            s_ref[g] = s_in * cd_x[:, gs] + _mm(bg_t, wgt[:, gs].astype(BF16))
            ys.append(y_g)
        y = jnp.concatenate(ys, axis=1)
        zc = z_ref[pl.ds(r0, q), :]
        y_ref[pl.ds(r0, q), :] = _rms(y * (zc * jax.nn.sigmoid(zc)), gssd_ref[...])
        return carry

    lax.fori_loop(0, tb // q, chunk, 0)

    @pl.when(t == pl.num_programs(1) - 1)
    def _():
        hl_ref[0] = s_ref[...]


def _ssd(xbc, dt, z, h0_t, cprev8, cw, cb, dtb, alog, dskip_x, gssd, expand, *, bsz, length, q, tb, n_heads,
         head_dim, n_state, n_groups):
    nt = length // tb
    width = n_heads * head_dim
    cdim = xbc.shape[1]
    row = lambda w: pl.BlockSpec((tb, w), lambda b, t: (b * nt + t, 0))
    gw = width // n_groups
    kern = functools.partial(_ssd_kernel, q=q, tb=tb, n_heads=n_heads, head_dim=head_dim, n_state=n_state,
                             n_groups=n_groups)
    return pl.pallas_call(
        kern, grid=(bsz, nt),
        in_specs=[row(cdim), row(LANES), row(width),
                  pl.BlockSpec((1, n_groups, n_state, gw), lambda b, t: (b, 0, 0, 0)),
                  pl.BlockSpec((1, 8, cdim), lambda b, t: (b, 0, 0)),
                  _full(cw.shape), _full(cb.shape), _full(dtb.shape), _full(alog.shape), _full(dskip_x.shape),
                  _full(gssd.shape), _full(expand.shape)],
        out_specs=[row(width),
                   pl.BlockSpec((1, n_groups, n_state, gw), lambda b, t: (b, 0, 0, 0)),
                   pl.BlockSpec((1, CONV_WIDTH - 1, cdim), lambda b, t: (b, 0, 0))],
        out_shape=[jax.ShapeDtypeStruct((bsz * length, width), F32),
                   jax.ShapeDtypeStruct((bsz, n_groups, n_state, gw), F32),
                   jax.ShapeDtypeStruct((bsz, CONV_WIDTH - 1, cdim), F32)],
        scratch_shapes=[pltpu.VMEM((tb + 8, cdim), F32), pltpu.VMEM((tb, cdim), F32),
                        pltpu.VMEM((n_groups, n_state, gw), F32)],
        compiler_params=_cparams("parallel", "arbitrary"),
    )(xbc, dt, z, h0_t, cprev8, cw, cb, dtb, alog, dskip_x, gssd, expand)


def _lambda(lq1_ref, lk1_ref, lq2_ref, lk2_ref, lam_init):
    e1 = jnp.exp(jnp.sum(lq1_ref[...] * lk1_ref[...], axis=1, keepdims=True))
    e2 = jnp.exp(jnp.sum(lq2_ref[...] * lk2_ref[...], axis=1, keepdims=True))
    return e1 - e2 + lam_init


def _split_components(qh, qk_dim):
    lane = lax.broadcasted_iota(I32, qh.shape, 1)
    zero = jnp.zeros_like(qh)
    return jnp.where(lane < qk_dim, qh, zero), jnp.where(lane >= qk_dim, qh, zero)


def _attn_prompt_kernel(q_ref, k_ref, vt_ref, lq1_ref, lk1_ref, lq2_ref, lk2_ref, gs_ref, o_ref,
                        acc1, acc2, m1, l1, m2, l2, *, tq, qk_dim, lam_init):
    qi = pl.program_id(2)
    qq1, qq2 = _split_components(q_ref[...], qk_dim)
    for acc, m, l in ((acc1, m1, l1), (acc2, m2, l2)):
        acc[...] = jnp.zeros_like(acc)
        m[...] = jnp.full_like(m, -jnp.inf)
        l[...] = jnp.zeros_like(l)
    kr = lax.broadcasted_iota(I32, (tq, tq), 0) // CHUNK
    qc = lax.broadcasted_iota(I32, (tq, tq), 1) // CHUNK
    visible = kr <= qc

    def step(j, masked):
        kk = k_ref[pl.ds(pl.multiple_of(j * tq, tq), tq), :]
        vt = vt_ref[j]
        for qq, acc, m, l in ((qq1, acc1, m1, l1), (qq2, acc2, m2, l2)):
            s = _mm_nt(kk, qq)
            if masked:
                s = jnp.where(visible, s, -jnp.inf)
            m_old = m[...]
            m_new = jnp.maximum(m_old, jnp.max(s, axis=0, keepdims=True))
            alpha = jnp.exp(m_old - m_new)
            p = jnp.exp(s - m_new)
            l[...] = alpha * l[...] + jnp.sum(p, axis=0, keepdims=True)
            acc[...] = alpha * acc[...] + _mm(vt, p.astype(BF16))
            m[...] = m_new

    def body(j, carry):
        step(j, False)
        return carry

    lax.fori_loop(0, qi, body, 0)
    step(qi, True)

    lam = _lambda(lq1_ref, lk1_ref, lq2_ref, lk2_ref, lam_init)
    o = acc1[...] * (1.0 / l1[...]) - lam * (acc2[...] * (1.0 / l2[...]))
    r = o * lax.rsqrt(jnp.mean(o * o, axis=0, keepdims=True) + EPS) * gs_ref[...]
    o_ref[...] = (r * (1.0 - lam_init)).T


def _attn_prompt(qb, kb, vt, lams, gs_col, *, tq, n_heads, qk_dim, lam_init):
    bsz, s, _ = qb.shape
    vd = vt.shape[3]
    nblk = s // tq
    kern = functools.partial(_attn_prompt_kernel, tq=tq, qk_dim=qk_dim, lam_init=lam_init)
    return pl.pallas_call(
        kern, grid=(bsz, n_heads, nblk),
        in_specs=[pl.BlockSpec((None, tq, 2 * qk_dim), lambda b, h, i: (b, i, h)),
                  pl.BlockSpec((None, s, 2 * qk_dim), lambda b, h, i: (b, 0, h)),
                  pl.BlockSpec((None, None, nblk, vd, tq), lambda b, h, i: (b, h, 0, 0, 0))]
                 + [_full(a.shape) for a in lams] + [_full(gs_col.shape)],
        out_specs=pl.BlockSpec((None, tq, vd), lambda b, h, i: (b, i, h)),
        out_shape=jax.ShapeDtypeStruct((bsz, s, n_heads * vd), F32),
        scratch_shapes=[pltpu.VMEM((vd, tq), F32), pltpu.VMEM((vd, tq), F32)] + [pltpu.VMEM((1, tq), F32)] * 4,
        compiler_params=_cparams("parallel", "parallel", "arbitrary"),
    )(qb, kb, vt, *lams, gs_col)


def _attn_sample_kernel(q_ref, kn_ref, vn_ref, kc_ref, vc_ref, lq1_ref, lk1_ref, lq2_ref, lk2_ref, gs_ref, o_ref,
                        *, n_heads, qk_dim, lam_init):
    lam = _lambda(lq1_ref, lk1_ref, lq2_ref, lk2_ref, lam_init)
    vd = 2 * qk_dim
    for h in range(n_heads):
        sl = slice(h * vd, (h + 1) * vd)
        qq = _split_components(q_ref[:, sl], qk_dim)
        kp = kc_ref[:, sl].astype(BF16)
        vp = vc_ref[:, sl].astype(BF16)
        kn = kn_ref[:, sl]
        vn = vn_ref[:, sl]
        outs = []
        for qc in qq:
            sp = _mm_nt(qc, kp)
            sn = _mm_nt(qc, kn)
            m = jnp.maximum(jnp.max(sp, axis=1, keepdims=True), jnp.max(sn, axis=1, keepdims=True))
            pp = jnp.exp(sp - m)
            pn = jnp.exp(sn - m)
            l = jnp.sum(pp, axis=1, keepdims=True) + jnp.sum(pn, axis=1, keepdims=True)
            outs.append((_mm(pp.astype(BF16), vp) + _mm(pn.astype(BF16), vn)) * (1.0 / l))
        o = outs[0] - lam * outs[1]
        o_ref[:, sl] = _rms(o, gs_ref[...]) * (1.0 - lam_init)


def _attn_sample(qb, kb, vb, kc, vc, lams, gs_row, *, n_heads, qk_dim, lam_init):
    bsz, length, w = qb.shape
    past = kc.shape[1]
    new = pl.BlockSpec((None, length, w), lambda b: (b, 0, 0))
    old = pl.BlockSpec((None, past, w), lambda b: (b, 0, 0))
    kern = functools.partial(_attn_sample_kernel, n_heads=n_heads, qk_dim=qk_dim, lam_init=lam_init)
    return pl.pallas_call(
        kern, grid=(bsz,),
        in_specs=[new, new, new, old, old] + [_full(a.shape) for a in lams] + [_full(gs_row.shape)],
        out_specs=new, out_shape=jax.ShapeDtypeStruct((bsz, length, w), F32),
        compiler_params=_cparams("parallel"),
    )(qb, kb, vb, kc, vc, *lams, gs_row)


def _mid_kernel(x_ref, ys_ref, oa_ref, woa_ref, wob_ref, gq_ref, wmq_ref, mk_ref, mv_ref, wmo_ref, x2_ref,
                *, n_heads, scale):
    x1 = x_ref[...] + _mm(ys_ref[...].astype(BF16), woa_ref[...]) + _mm(oa_ref[...].astype(BF16), wob_ref[...])
    qm = _mm(_rms(x1, gq_ref[...]).astype(BF16), wmq_ref[...])
    hd = qm.shape[1] // n_heads
    oms = []
    for h in range(n_heads):
        sl = slice(h * hd, (h + 1) * hd)
        s = _mm_nt(qm[:, sl].astype(BF16), mk_ref[:, sl].astype(BF16)) * scale
        p = jnp.exp(s - jnp.max(s, axis=1, keepdims=True))
        p = p * (1.0 / jnp.sum(p, axis=1, keepdims=True))
        oms.append(_mm(p.astype(BF16), mv_ref[:, sl].astype(BF16)))
    om = jnp.concatenate(oms, axis=1)
    x2_ref[...] = x1 + _mm(om.astype(BF16), wmo_ref[...])


def _mid(x, ys, oa, woa, wob, gq, wmq, mk, mv, wmo, *, tm, rows_per_batch, n_heads):
    n, d = x.shape
    tiles_per_batch = rows_per_batch // tm
    n_mem = mk.shape[1]
    row = lambda w: pl.BlockSpec((tm, w), lambda i: (i, 0))
    mem = pl.BlockSpec((None, n_mem, d), lambda i: (i // tiles_per_batch, 0, 0))
    kern = functools.partial(_mid_kernel, n_heads=n_heads, scale=1.0 / math.sqrt(d // n_heads))
    return pl.pallas_call(
        kern, grid=(n // tm,),
        in_specs=[row(d), row(ys.shape[1]), row(oa.shape[1]), _full(woa.shape), _full(wob.shape), _full(gq.shape),
                  _full(wmq.shape), mem, mem, _full(wmo.shape)],
        out_specs=row(d), out_shape=jax.ShapeDtypeStruct((n, d), F32),
        compiler_params=_cparams("parallel"),
    )(x, ys, oa, woa, wob, gq, wmq, mk, mv, wmo)


def _topk_rows(s, k):
    n, t = s.shape
    rows = lax.broadcasted_iota(I32, (n, t), 0)
    slot = lax.broadcasted_iota(I32, (k, t), 0)
    vals = jnp.zeros((k, t), F32)
    idxs = jnp.zeros((k, t), I32)
    for i in range(k):
        m = jnp.max(s, axis=0, keepdims=True)
        idx = jnp.min(jnp.where(s == m, rows, n), axis=0, keepdims=True)
        s = jnp.where(rows == idx, -jnp.inf, s)
        vals = jnp.where(slot == i, m, vals)
        idxs = jnp.where(slot == i, idx, idxs)
    return vals, idxs


def _pair_candidates(k):
    rows = [0 * k + j for j in range(k)]
    rows += [1 * k + j for j in range(8)]
    for i in range(2, 8):
        rows += [(i * k + j) if (i + 1) * (j + 1) <= k else -1 for j in range(8)]
    rows += [i * k for i in range(8, k)]
    return rows


def _peer_score_kernel(x2_ref, gf_ref, wpqt_ref, keys_ref, fid_ref, eid_ref, gate_ref, qt_ref, e_s, g_s,
                       *, n_heads, n_keys, topk):
    hf = _rms(x2_ref[...], gf_ref[...]).astype(BF16)
    qt_ref[...] = _mm_nt(wpqt_ref[...], hf).astype(BF16)
    half = wpqt_ref.shape[0] // (2 * n_heads)
    fid = fid_ref[...]
    tm = fid.shape[1]
    slot = lax.broadcasted_iota(I32, (topk, tm), 0)

    def head(h, carry):
        tops = []
        for c in range(2):
            r0 = pl.multiple_of((2 * h + c) * half, half)
            tops.append(_topk_rows(_mm(keys_ref[h, c], qt_ref[pl.ds(r0, half), :]), topk))
        (s1, i1), (s2, i2) = tops
        blocks_s = [s1[0:1] + s2, s1[1:2] + s2[0:8]]
        blocks_e = [i1[0:1] * n_keys + i2, i1[1:2] * n_keys + i2[0:8]]
        for i in range(2, 8):
            blocks_s.append(s1[i:i + 1] + s2[0:8])
            blocks_e.append(i1[i:i + 1] * n_keys + i2[0:8])
        blocks_s.append(s1[8:topk] + s2[0:1])
        blocks_e.append(i1[8:topk] * n_keys + i2[0:1])
        cand = jnp.where(fid >= 0, jnp.concatenate(blocks_s, axis=0), -jnp.inf)
        eid = jnp.concatenate(blocks_e, axis=0)
        sv = jnp.zeros((topk, tm), F32)
        ev = jnp.zeros((topk, tm), I32)
        for i in range(topk):
            m = jnp.max(cand, axis=0, keepdims=True)
            f = jnp.min(jnp.where(cand == m, fid, topk * topk), axis=0, keepdims=True)
            sel = fid == f
            e = jnp.max(jnp.where(sel, eid, -1), axis=0, keepdims=True)
            cand = jnp.where(sel, -jnp.inf, cand)
            sv = jnp.where(slot == i, m, sv)
            ev = jnp.where(slot == i, e, ev)
        p = jnp.exp(sv - sv[0:1])
        r0 = pl.multiple_of(h * topk, topk)
        e_s[pl.ds(r0, topk), :] = ev
        g_s[pl.ds(r0, topk), :] = p * (1.0 / jnp.sum(p, axis=0, keepdims=True))
        return carry

    lax.fori_loop(0, n_heads, head, 0)
    eid_ref[...] = e_s[...].T
    gate_ref[...] = g_s[...].T


def _peer_score(x2, gf, wpqt, keys, *, tm):
    n, d = x2.shape
    n_heads, _, n_keys, _ = keys.shape
    picks = n_heads * PEER_TOPK
    fid = jnp.tile(jnp.asarray(_pair_candidates(PEER_TOPK), I32)[:, None], (1, tm))
    kern = functools.partial(_peer_score_kernel, n_heads=n_heads, n_keys=n_keys, topk=PEER_TOPK)
    row = lambda w: pl.BlockSpec((tm, w), lambda i: (i, 0))
    return pl.pallas_call(
        kern, grid=(n // tm,),
        in_specs=[row(d), _full(gf.shape), _full(wpqt.shape), _full(keys.shape), _full(fid.shape)],
        out_specs=[row(picks), row(picks)],
        out_shape=[jax.ShapeDtypeStruct((n, picks), I32), jax.ShapeDtypeStruct((n, picks), F32)],
        scratch_shapes=[pltpu.VMEM((wpqt.shape[0], tm), BF16), pltpu.VMEM((picks, tm), I32),
                        pltpu.VMEM((picks, tm), F32)],
        compiler_params=_cparams("parallel"),
    )(x2, gf, wpqt, keys, fid)


def _gelu_tanh(x):
    return 0.5 * x * (1.0 + jnp.tanh(math.sqrt(2.0 / math.pi) * (x + 0.044715 * (x * x * x))))


def _peer_gather_kernel(eid_ref, eid_next_ref, gate_ref, x2_ref, gf_ref, gfin_ref, tab_ref, y_ref, buf0, buf1,
                        sems, *, tt, picks):
    d = x2_ref.shape[1]
    i = pl.program_id(0)
    bufs = (buf0, buf1)
    group = 2

    def issue(idx_ref, row, s, t):
        for k in range(picks):
            e = idx_ref[row, k]
            pltpu.make_async_copy(tab_ref.at[pl.ds(e, 1), :], bufs[s].at[t, pl.ds(k, 1), :], sems.at[s, t]).start()

    def wait(s, t):
        pltpu.make_async_copy(tab_ref.at[pl.ds(0, picks), :], bufs[s].at[t], sems.at[s, t]).wait()

    def compute(s, t, row):
        x2 = x2_ref[row:row + 1, :]
        hf = _rms(x2, gf_ref[...])
        rows = bufs[s][t]
        a = jnp.sum(rows[:, :d] * hf, axis=1, keepdims=True)
        act = _gelu_tanh(a) * gate_ref[:, row:row + 1]
        o = jnp.sum(rows[:, d:] * act, axis=0, keepdims=True)
        y_ref[row:row + 1, :] = _rms(x2 + o, gfin_ref[...])

    @pl.when(i == 0)
    def _():
        def token(t, carry):
            issue(eid_ref, t, 0, t)
            return carry

        lax.fori_loop(0, tt, token, 0)

    for s, idx_ref, idx_row0, out_row0 in ((0, eid_ref, tt, 0), (1, eid_next_ref, 0, tt)):
        for t0 in range(0, tt, group):
            for t in range(t0, t0 + group):
                wait(s, t)
            for t in range(t0, t0 + group):
                issue(idx_ref, idx_row0 + t, 1 - s, t)
            for t in range(t0, t0 + group):
                compute(s, t, out_row0 + t)

    @pl.when(i == pl.num_programs(0) - 1)
    def _():
        for t in range(tt):
            wait(0, t)


def _peer_gather(eid, gate_t, x2, gf, gfin, tab, *, tt):
    n, d = x2.shape
    picks = eid.shape[1]
    steps = n // (2 * tt)
    kern = functools.partial(_peer_gather_kernel, tt=tt, picks=picks)
    return pl.pallas_call(
        kern, grid=(steps,),
        in_specs=[pl.BlockSpec((2 * tt, picks), lambda i: (i, 0), memory_space=pltpu.SMEM),
                  pl.BlockSpec((tt, picks), lambda i: (jnp.minimum(2 * i + 2, 2 * steps - 1), 0),
                               memory_space=pltpu.SMEM),
                  pl.BlockSpec((None, picks, 2 * tt), lambda i: (i, 0, 0)),
                  pl.BlockSpec((2 * tt, d), lambda i: (i, 0)), _full(gf.shape), _full(gfin.shape),
                  pl.BlockSpec(memory_space=pl.ANY)],
        out_specs=pl.BlockSpec((2 * tt, d), lambda i: (i, 0)),
        out_shape=jax.ShapeDtypeStruct((n, d), F32),
        scratch_shapes=[pltpu.VMEM((tt, picks, tab.shape[1]), F32), pltpu.VMEM((tt, picks, tab.shape[1]), F32),
                        pltpu.SemaphoreType.DMA((2, tt))],
        compiler_params=_cparams("arbitrary"),
    )(eid, eid, gate_t, x2, gf, gfin, tab)


def _layer(x, pos_offset, k_past, v_past, h0, conv_prev, mem_k, mem_v, p, lam_init, dims, tiles):
    bsz, length, d = x.shape
    n = bsz * length
    nh, hd, ns, ng = dims["ssd_heads"], dims["ssd_head_dim"], dims["ssd_state"], dims["ssd_groups"]
    ah, qk = dims["diff_heads"], dims["qk_dim"]
    vd = 2 * qk
    x2d = x.reshape(n, d)

    tables = _rope_tables(length, pos_offset, qk)
    tm = tiles["tm"]
    if tm > length:
        tables = tuple(jnp.tile(t, (tm // length, 1)) for t in tables)
        cycle = tm
    else:
        cycle = length
    z, xbc, dt, qb, k, v, kb, vb = _in_proj(x2d, p["g_mix"], p["w_in_parts"], tables, tm, cycle, 1.0 / math.sqrt(qk))

    gw = nh * hd // ng
    h0_t = h0.reshape(bsz, ng, nh // ng, hd, ns).transpose(0, 1, 4, 2, 3).reshape(bsz, ng, ns, gw)
    cprev8 = jnp.pad(conv_prev, ((0, 0), (8 - (CONV_WIDTH - 1), 0), (0, 0)))
    q_len = min(CHUNK, length)
    y_ssd, h_last_t, conv_new = _ssd(xbc, dt, z, h0_t, cprev8, p["conv_w"], p["conv_b"], p["dt_bias_pad"],
                                     p["a_log_pad"], p["d_skip_x"], p["g_ssd"], p["expand"], bsz=bsz, length=length,
                                     q=q_len, tb=tiles["ssd_tb"], n_heads=nh, head_dim=hd, n_state=ns, n_groups=ng)
    h_last = h_last_t.reshape(bsz, ng, ns, nh // ng, hd).transpose(0, 1, 3, 4, 2).reshape(bsz, nh, hd, ns)

    lams = (p["lam_q1"], p["lam_k1"], p["lam_q2"], p["lam_k2"])
    if k_past is None:
        tq = tiles["attn_tq"]
        vt = vb.reshape(bsz, length // tq, tq, ah, vd).transpose(0, 3, 1, 4, 2)
        o = _attn_prompt(qb.reshape(bsz, length, ah * vd), kb.reshape(bsz, length, ah * vd), vt, lams,
                         p["g_subln"].reshape(vd, 1), tq=tq, n_heads=ah, qk_dim=qk, lam_init=lam_init)
    else:
        past = k_past.shape[1]
        o = _attn_sample(qb.reshape(bsz, length, ah * vd), kb.reshape(bsz, length, ah * vd),
                         vb.reshape(bsz, length, ah * vd), k_past.reshape(bsz, past, ah * vd),
                         v_past.reshape(bsz, past, ah * vd), lams, p["g_subln"].reshape(1, vd), n_heads=ah,
                         qk_dim=qk, lam_init=lam_init)

    n_mem = mem_k.shape[1]
    x2 = _mid(x2d, y_ssd, o.reshape(n, ah * vd), p["w_out_a"], p["w_out_b"], p["g_mem_q"], p["w_mq"],
              mem_k.reshape(bsz, n_mem, d), mem_v.reshape(bsz, n_mem, d), p["w_mo"], tm=tiles["mid_tm"],
              rows_per_batch=length, n_heads=dims["mem_heads"])

    eid, gate = _peer_score(x2, p["g_ffn"], p["w_pq_t"], p["peer_keys"], tm=tiles["peer_tm"])
    tt = tiles["gather_tt"]
    picks = eid.shape[1]
    gate_t = gate.reshape(n // (2 * tt), 2 * tt, picks).transpose(0, 2, 1)
    y = _peer_gather(eid, gate_t, x2, p["g_ffn"], p["g_final"], p["peer_tab"], tt=tt)
    return (y.reshape(bsz, length, d), k.reshape(bsz, length, ah, vd), v.reshape(bsz, length, ah, vd), h_last,
            conv_new)


def kernel(x_prompt, x_sample, cache_attn_k, cache_attn_v, cache_mem_k, cache_mem_v, state_ssm, state_conv, mem_prompt, g_mix, w_in, conv_w, conv_b, dt_bias, a_log, d_skip, g_ssd, lam_q1, lam_k1, lam_q2, lam_k2, g_subln, w_out, g_mem_q, g_mem_kv, w_mq, w_mk, w_mv, w_mo, g_ffn, w_pq, peer_keys, peer_u, peer_v, g_final):
    depth = w_in.shape[0]
    assert depth == 1, "the final norm is fused into the last layer's PEER kernel; one layer supported"
    bp, seq, d = x_prompt.shape
    _, _, nh, hd, ns = state_ssm.shape
    cdim = state_conv.shape[-1]
    width = nh * hd
    ng = (cdim - width) // (2 * ns)
    ah = cache_attn_k.shape[3]
    qk = lam_q1.shape[-1]
    dims = dict(ssd_heads=nh, ssd_head_dim=hd, ssd_state=ns, ssd_groups=ng, diff_heads=ah, qk_dim=qk,
                mem_heads=cache_mem_k.shape[3])
    sizes = (width, cdim, nh, ah * 2 * qk, ah * 2 * qk, ah * 2 * qk)
    splits = [sum(sizes[:i + 1]) for i in range(len(sizes) - 1)]
    n_mem = mem_prompt.shape[1]

    l = 0
    lam_init = 0.8 - 0.6 * math.exp(-0.3 * l)
    wz, wx, wdt, wq, wk, wv = jnp.split(w_in[l].astype(BF16), splits, axis=1)
    wdt = jnp.pad(wdt, ((0, 0), (0, LANES - nh)))
    pad_h = lambda a: jnp.pad(a.reshape(1, nh), ((0, 0), (0, LANES - nh)))
    expand = (jnp.arange(LANES)[:, None] == (jnp.arange(width) // hd)[None, :]).astype(F32)
    p = dict(
        g_mix=g_mix[l].reshape(1, d), w_in_parts=(wz, wx, wdt, wq, wk, wv),
        conv_w=conv_w[l], conv_b=conv_b[l].reshape(1, cdim), dt_bias_pad=pad_h(dt_bias[l]), a_log_pad=pad_h(a_log[l]),
        d_skip_x=jnp.repeat(d_skip[l], hd).reshape(1, width), g_ssd=g_ssd[l].reshape(1, width), expand=expand,
        lam_q1=lam_q1[l].reshape(1, qk), lam_k1=lam_k1[l].reshape(1, qk), lam_q2=lam_q2[l].reshape(1, qk),
        lam_k2=lam_k2[l].reshape(1, qk), g_subln=g_subln[l],
        w_out_a=w_out[l, :width].astype(BF16), w_out_b=w_out[l, width:].astype(BF16),
        g_mem_q=g_mem_q[l].reshape(1, d), w_mq=w_mq[l].astype(BF16), w_mo=w_mo[l].astype(BF16),
        g_ffn=g_ffn[l].reshape(1, d), w_pq_t=w_pq[l].T.astype(BF16), peer_keys=peer_keys[l].astype(BF16),
        peer_tab=jnp.concatenate([peer_u[l], peer_v[l]], axis=1), g_final=g_final.reshape(1, d),
    )

    mk_p, mv_p = _norm_mm2(mem_prompt.reshape(bp * n_mem, d), g_mem_kv[l].reshape(1, d), w_mk[l].astype(BF16),
                           w_mv[l].astype(BF16), tm=n_mem)
    mem_shape = (bp, n_mem) + cache_mem_k.shape[3:]
    mk_p, mv_p = mk_p.reshape(mem_shape), mv_p.reshape(mem_shape)

    tiles_p = dict(tm=min(512, seq), ssd_tb=min(512, seq), attn_tq=min(512, seq), mid_tm=min(512, seq),
                   peer_tm=256, gather_tt=8)
    h0_p = jnp.zeros((bp, nh, hd, ns), F32)
    conv0_p = jnp.zeros((bp, CONV_WIDTH - 1, cdim), F32)
    yp, kp, vp, hp, cp = _layer(x_prompt, 0, None, None, h0_p, conv0_p, mk_p, mv_p, p, lam_init, dims, tiles_p)

    bs, dseq, _ = x_sample.shape
    past = cache_attn_k.shape[2]
    ns_rows = bs * dseq
    tiles_s = dict(tm=min(512, ns_rows), ssd_tb=dseq, mid_tm=dseq, peer_tm=min(256, ns_rows), gather_tt=8)
    ys, ksn, vsn, hsn, csn = _layer(x_sample, past, cache_attn_k[l], cache_attn_v[l], state_ssm[l], state_conv[l],
                                    cache_mem_k[l], cache_mem_v[l], p, lam_init, dims, tiles_s)

    st = lambda a: a[None]
    return (yp, ys, st(kp), st(vp), st(hp), st(cp), st(mk_p), st(mv_p), st(ksn), st(vsn), st(hsn), st(csn))
```

```python
import functools
import math

import jax
import jax.numpy as jnp
from jax import lax
from jax.experimental import pallas as pl
from jax.experimental.pallas import tpu as pltpu

F32 = jnp.float32
BF16 = jnp.bfloat16
I32 = jnp.int32
EPS = 1e-6
LANES = 128
CHUNK = 64
ROT_HALF = 8
ROPE_THETA = 500000.0
CONV_WIDTH = 4
PEER_TOPK = 16
VMEM_LIMIT = 56 * 1024 * 1024
HIGHEST = lax.Precision.HIGHEST
NT_DIMS = (((1,), (1,)), ((), ()))


def _cparams(*sem):
    return pltpu.CompilerParams(dimension_semantics=sem, vmem_limit_bytes=VMEM_LIMIT)


def _rms(x, g):
    return x * lax.rsqrt(jnp.mean(x * x, axis=-1, keepdims=True) + EPS) * g


def _mm(a, b):
    return jnp.dot(a, b, preferred_element_type=F32)


def _mm_nt(a, b):
    return lax.dot_general(a, b, NT_DIMS, preferred_element_type=F32)


def _full(shape):
    return pl.BlockSpec(shape, lambda *_: (0,) * len(shape))


def _in_proj_kernel(x_ref, g_ref, wz_ref, wx_ref, wdt_ref, wq_ref, wk_ref, wv_ref, c_ref, s1_ref, s2_ref,
                    z_ref, xbc_ref, dt_ref, q_ref, k_ref, v_ref, kb_ref, vb_ref, *, q_scale):
    h = _rms(x_ref[...], g_ref[...]).astype(BF16)
    z_ref[...] = _mm(h, wz_ref[...])
    xbc_ref[...] = _mm(h, wx_ref[...])
    dt_ref[...] = _mm(h, wdt_ref[...])
    c, s1, s2 = c_ref[...], s1_ref[...], s2_ref[...]

    def rope(t):
        parts = []
        for gi in range(t.shape[1] // LANES):
            tg = t[:, gi * LANES:(gi + 1) * LANES]
            parts.append(tg * c + pltpu.roll(tg, ROT_HALF, 1) * s1 + pltpu.roll(tg, LANES - ROT_HALF, 1) * s2)
        return jnp.concatenate(parts, axis=1)

    q_ref[...] = (rope(_mm(h, wq_ref[...])) * q_scale).astype(BF16)
    k = rope(_mm(h, wk_ref[...]))
    k_ref[...] = k
    kb_ref[...] = k.astype(BF16)
    v = _mm(h, wv_ref[...])
    v_ref[...] = v
    vb_ref[...] = v.astype(BF16)


def _rope_tables(length, offset, qk_dim):
    inv = 1.0 / (ROPE_THETA ** (jnp.arange(ROT_HALF, dtype=F32) / ROT_HALF))
    pos = jnp.arange(length, dtype=F32) + offset
    ang = pos[:, None] * inv[None, :]
    cos, sin = jnp.cos(ang), jnp.sin(ang)
    rest = qk_dim - 2 * ROT_HALF
    one, zero = jnp.ones((length, rest), F32), jnp.zeros((length, rest), F32)
    z8 = jnp.zeros((length, ROT_HALF), F32)
    rep = LANES // qk_dim
    c = jnp.tile(jnp.concatenate([cos, cos, one], axis=1), (1, rep))
    s1 = jnp.tile(jnp.concatenate([z8, sin, zero], axis=1), (1, rep))
    s2 = jnp.tile(jnp.concatenate([-sin, z8, zero], axis=1), (1, rep))
    return c, s1, s2


def _in_proj(x, g, ws, tables, tm, rows_per_table_cycle, q_scale):
    n, d = x.shape
    wz, wx, wdt, wq, wk, wv = ws
    nt = rows_per_table_cycle // tm
    row = lambda w: pl.BlockSpec((tm, w), lambda i: (i, 0))
    tab = pl.BlockSpec((tm, LANES), lambda i: (i % nt, 0))
    widths = [wz.shape[1], wx.shape[1], wdt.shape[1], wq.shape[1], wk.shape[1], wv.shape[1]]
    out_shape = [jax.ShapeDtypeStruct((n, w), F32) for w in widths[:3]]
    out_shape += [jax.ShapeDtypeStruct((n, widths[3]), BF16), jax.ShapeDtypeStruct((n, widths[4]), F32),
                  jax.ShapeDtypeStruct((n, widths[5]), F32), jax.ShapeDtypeStruct((n, widths[4]), BF16),
                  jax.ShapeDtypeStruct((n, widths[5]), BF16)]
    out_specs = [row(w) for w in widths[:3]] + [row(widths[3]), row(widths[4]), row(widths[5]), row(widths[4]),
                                                 row(widths[5])]
    return pl.pallas_call(
        functools.partial(_in_proj_kernel, q_scale=q_scale),
        grid=(n // tm,),
        in_specs=[row(d), _full((1, d))] + [_full(w.shape) for w in ws] + [tab, tab, tab],
        out_specs=out_specs, out_shape=out_shape,
        compiler_params=_cparams("parallel"),
    )(x, g, *ws, *tables)


def _norm_mm2_kernel(x_ref, g_ref, wa_ref, wb_ref, a_ref, b_ref):
    h = _rms(x_ref[...], g_ref[...]).astype(BF16)
    a_ref[...] = _mm(h, wa_ref[...])
    b_ref[...] = _mm(h, wb_ref[...])


def _norm_mm2(x, g, wa, wb, tm):
    n, d = x.shape
    row = lambda w: pl.BlockSpec((tm, w), lambda i: (i, 0))
    return pl.pallas_call(
        _norm_mm2_kernel, grid=(n // tm,),
        in_specs=[row(d), _full((1, d)), _full(wa.shape), _full(wb.shape)],
        out_specs=[row(wa.shape[1]), row(wb.shape[1])],
        out_shape=[jax.ShapeDtypeStruct((n, wa.shape[1]), F32), jax.ShapeDtypeStruct((n, wb.shape[1]), F32)],
        compiler_params=_cparams("parallel"),
    )(x, g, wa, wb)


def _pad_rows_to_lanes(x):
    q = x.shape[0]
    if q == LANES:
        return x
    return jnp.concatenate([x, jnp.zeros((LANES - q, x.shape[1]), x.dtype)], axis=0)


def _ssd_kernel(xbc_ref, dt_ref, z_ref, h0_ref, cprev_ref, cw_ref, cb_ref, dtb_ref, alog_ref, dskip_ref, gssd_ref,
                expand_ref, y_ref, hl_ref, cn_ref, xp_ref, xc_ref, s_ref, *, q, tb, n_heads, head_dim, n_state,
                n_groups):
    t = pl.program_id(1)
    width = n_heads * head_dim
    gw = width // n_groups
    hpg = n_heads // n_groups
    halo = 8

    @pl.when(t == 0)
    def _():
        s_ref[...] = h0_ref[0]
        xp_ref[0:halo, :] = cprev_ref[0]

    xp_ref[halo:halo + tb, :] = xbc_ref[...]
    cw = cw_ref[...]
    conv = cb_ref[...]
    for j in range(CONV_WIDTH):
        off = halo - (CONV_WIDTH - 1) + j
        conv = conv + cw[j:j + 1, :] * xp_ref[off:off + tb, :]
    xc_ref[...] = conv * jax.nn.sigmoid(conv)
    cn_ref[0] = xp_ref[halo + tb - (CONV_WIDTH - 1):halo + tb, :]
    xp_ref[0:halo, :] = xp_ref[tb:tb + halo, :]

    a_neg = -jnp.exp(alog_ref[...])
    ri = lax.broadcasted_iota(I32, (q, q), 0)
    ci = lax.broadcasted_iota(I32, (q, q), 1)
    tril = ci <= ri
    tril_f = tril.astype(F32)
    lane_g = lax.broadcasted_iota(I32, (1, gw), 1)

    def chunk(c, carry):
        r0 = pl.multiple_of(c * q, q)
        xc = xc_ref[pl.ds(r0, q), :]
        xs = xc[:, :width]
        bm = xc[:, width:width + n_groups * n_state]
        cm = xc[:, width + n_groups * n_state:]
        dtr = dt_ref[pl.ds(r0, q), :] + dtb_ref[...]
        dt = jnp.maximum(dtr, 0.0) + jnp.log1p(jnp.exp(-jnp.abs(dtr)))
        da = dt * a_neg
        a_cum = jnp.dot(tril_f, da, precision=HIGHEST, preferred_element_type=F32)
        a_last = a_cum[q - 1:q, :]
        stk = jnp.concatenate([dt, jnp.exp(a_last - a_cum), jnp.exp(a_cum)], axis=0)
        ex = jnp.dot(stk, expand_ref[...], precision=HIGHEST, preferred_element_type=F32)
        dt_x, te_x, ec_x = ex[:q], ex[q:2 * q], ex[2 * q:]
        cd_x = ec_x[q - 1:q, :]
        xdt = xs * dt_x
        wgt = xdt * te_x
        a_cum_t = _pad_rows_to_lanes(a_cum).T
        ys = []
        for g in range(n_groups):
            bg = bm[:, g * n_state:(g + 1) * n_state]
            cg = cm[:, g * n_state:(g + 1) * n_state].astype(BF16)
            bg_t = _pad_rows_to_lanes(bg).T[:, :q].astype(BF16)
            cb = _mm_nt(cg, bg.astype(BF16))
            s_in = s_ref[g]
            gs = slice(g * gw, (g + 1) * gw)
            y_g = _mm(cg, s_in.astype(BF16)) * ec_x[:, gs] + dskip_ref[:, gs] * xs[:, gs]
            xdt_g = xdt[:, gs]
            for hl in range(hpg):
                h = g * hpg + hl
                seg = a_cum[:, h:h + 1] - a_cum_t[h:h + 1, :q]
                m_h = (cb * jnp.exp(jnp.where(tril, seg, -jnp.inf))).astype(BF16)
                in_head = (lane_g >= hl * head_dim) & (lane_g < (hl + 1) * head_dim)
                y_g = y_g + _mm(m_h, jnp.where(in_head, xdt_g, 0.0).astype(BF16))
            s_ref[g] = s_in * cd_x[:, gs] + _mm(bg_t, wgt[:, gs].astype(BF16))
            ys.append(y_g)
        y = jnp.concatenate(ys, axis=1)
        zc = z_ref[pl.ds(r0, q), :]
        y_ref[pl.ds(r0, q), :] = _rms(y * (zc * jax.nn.sigmoid(zc)), gssd_ref[...])
        return carry

    lax.fori_loop(0, tb // q, chunk, 0)

    @pl.when(t == pl.num_programs(1) - 1)
    def _():
        hl_ref[0] = s_ref[...]


def _ssd(xbc, dt, z, h0_t, cprev8, cw, cb, dtb, alog, dskip_x, gssd, expand, *, bsz, length, q, tb, n_heads,
         head_dim, n_state, n_groups):
    nt = length // tb
    width = n_heads * head_dim
    cdim = xbc.shape[1]
    row = lambda w: pl.BlockSpec((tb, w), lambda b, t: (b * nt + t, 0))
    gw = width // n_groups
    kern = functools.partial(_ssd_kernel, q=q, tb=tb, n_heads=n_heads, head_dim=head_dim, n_state=n_state,
                             n_groups=n_groups)
    return pl.pallas_call(
        kern, grid=(bsz, nt),
        in_specs=[row(cdim), row(LANES), row(width),
                  pl.BlockSpec((1, n_groups, n_state, gw), lambda b, t: (b, 0, 0, 0)),
                  pl.BlockSpec((1, 8, cdim), lambda b, t: (b, 0, 0)),
                  _full(cw.shape), _full(cb.shape), _full(dtb.shape), _full(alog.shape), _full(dskip_x.shape),
                  _full(gssd.shape), _full(expand.shape)],
        out_specs=[row(width),
                   pl.BlockSpec((1, n_groups, n_state, gw), lambda b, t: (b, 0, 0, 0)),
                   pl.BlockSpec((1, CONV_WIDTH - 1, cdim), lambda b, t: (b, 0, 0))],
        out_shape=[jax.ShapeDtypeStruct((bsz * length, width), F32),
                   jax.ShapeDtypeStruct((bsz, n_groups, n_state, gw), F32),
                   jax.ShapeDtypeStruct((bsz, CONV_WIDTH - 1, cdim), F32)],
        scratch_shapes=[pltpu.VMEM((tb + 8, cdim), F32), pltpu.VMEM((tb, cdim), F32),
                        pltpu.VMEM((n_groups, n_state, gw), F32)],
        compiler_params=_cparams("parallel", "arbitrary"),
    )(xbc, dt, z, h0_t, cprev8, cw, cb, dtb, alog, dskip_x, gssd, expand)


def _lambda(lq1_ref, lk1_ref, lq2_ref, lk2_ref, lam_init):
    e1 = jnp.exp(jnp.sum(lq1_ref[...] * lk1_ref[...], axis=1, keepdims=True))
    e2 = jnp.exp(jnp.sum(lq2_ref[...] * lk2_ref[...], axis=1, keepdims=True))
    return e1 - e2 + lam_init


def _split_components(qh, qk_dim):
    lane = lax.broadcasted_iota(I32, qh.shape, 1)
    zero = jnp.zeros_like(qh)
    return jnp.where(lane < qk_dim, qh, zero), jnp.where(lane >= qk_dim, qh, zero)


def _attn_prompt_kernel(q_ref, k_ref, vt_ref, lq1_ref, lk1_ref, lq2_ref, lk2_ref, gs_ref, o_ref,
                        acc1, acc2, m1, l1, m2, l2, s1_ref, s2_ref, *, tq, qk_dim, lam_init):
    qi = pl.program_id(2)
    qq1, qq2 = _split_components(q_ref[...], qk_dim)
    for acc, m, l in ((acc1, m1, l1), (acc2, m2, l2)):
        acc[...] = jnp.zeros_like(acc)
        m[...] = jnp.full_like(m, -jnp.inf)
        l[...] = jnp.zeros_like(l)

    def scores(j):
        kk = k_ref[pl.ds(pl.multiple_of(j * tq, tq), tq), :]
        return _mm_nt(kk, qq1), _mm_nt(kk, qq2)

    def consume(j, s_pair):
        vt = vt_ref[j]
        for s, acc, m, l in ((s_pair[0], acc1, m1, l1), (s_pair[1], acc2, m2, l2)):
            m_old = m[...]
            m_new = jnp.maximum(m_old, jnp.max(s, axis=0, keepdims=True))
            alpha = jnp.exp(m_old - m_new)
            p = jnp.exp(s - m_new)
            l[...] = alpha * l[...] + jnp.sum(p, axis=0, keepdims=True)
            acc[...] = alpha * acc[...] + _mm(vt, p.astype(BF16))
            m[...] = m_new

    s1_ref[...], s2_ref[...] = scores(0)

    def body(j, carry):
        cur = (s1_ref[...], s2_ref[...])
        nxt = scores(j + 1)
        consume(j, cur)
        s1_ref[...], s2_ref[...] = nxt
        return carry

    lax.fori_loop(0, qi, body, 0)
    kr = lax.broadcasted_iota(I32, (tq, tq), 0) // CHUNK
    qc = lax.broadcasted_iota(I32, (tq, tq), 1) // CHUNK
    visible = kr <= qc
    consume(qi, (jnp.where(visible, s1_ref[...], -jnp.inf), jnp.where(visible, s2_ref[...], -jnp.inf)))

    lam = _lambda(lq1_ref, lk1_ref, lq2_ref, lk2_ref, lam_init)
    o = acc1[...] * (1.0 / l1[...]) - lam * (acc2[...] * (1.0 / l2[...]))
    r = o * lax.rsqrt(jnp.mean(o * o, axis=0, keepdims=True) + EPS) * gs_ref[...]
    o_ref[...] = (r * (1.0 - lam_init)).T


def _attn_prompt(qb, kb, vt, lams, gs_col, *, tq, n_heads, qk_dim, lam_init):
    bsz, s, _ = qb.shape
    vd = vt.shape[3]
    nblk = s // tq
    kern = functools.partial(_attn_prompt_kernel, tq=tq, qk_dim=qk_dim, lam_init=lam_init)
    return pl.pallas_call(
        kern, grid=(bsz, n_heads, nblk),
        in_specs=[pl.BlockSpec((None, tq, 2 * qk_dim), lambda b, h, i: (b, i, h)),
                  pl.BlockSpec((None, s, 2 * qk_dim), lambda b, h, i: (b, 0, h)),
                  pl.BlockSpec((None, None, nblk, vd, tq), lambda b, h, i: (b, h, 0, 0, 0))]
                 + [_full(a.shape) for a in lams] + [_full(gs_col.shape)],
        out_specs=pl.BlockSpec((None, tq, vd), lambda b, h, i: (b, i, h)),
        out_shape=jax.ShapeDtypeStruct((bsz, s, n_heads * vd), F32),
        scratch_shapes=[pltpu.VMEM((vd, tq), F32), pltpu.VMEM((vd, tq), F32)] + [pltpu.VMEM((1, tq), F32)] * 4
                       + [pltpu.VMEM((tq, tq), F32)] * 2,
        compiler_params=_cparams("parallel", "parallel", "arbitrary"),
    )(qb, kb, vt, *lams, gs_col)


def _attn_sample_kernel(q_ref, kn_ref, vn_ref, kc_ref, vc_ref, lq1_ref, lk1_ref, lq2_ref, lk2_ref, gs_ref, o_ref,
                        *, n_heads, qk_dim, lam_init):
    lam = _lambda(lq1_ref, lk1_ref, lq2_ref, lk2_ref, lam_init)
    vd = 2 * qk_dim
    for h in range(n_heads):
        sl = slice(h * vd, (h + 1) * vd)
        qq = _split_components(q_ref[:, sl], qk_dim)
        kp = kc_ref[:, sl].astype(BF16)
        vp = vc_ref[:, sl].astype(BF16)
        kn = kn_ref[:, sl]
        vn = vn_ref[:, sl]
        outs = []
        for qc in qq:
            sp = _mm_nt(qc, kp)
            sn = _mm_nt(qc, kn)
            m = jnp.maximum(jnp.max(sp, axis=1, keepdims=True), jnp.max(sn, axis=1, keepdims=True))
            pp = jnp.exp(sp - m)
            pn = jnp.exp(sn - m)
            l = jnp.sum(pp, axis=1, keepdims=True) + jnp.sum(pn, axis=1, keepdims=True)
            outs.append((_mm(pp.astype(BF16), vp) + _mm(pn.astype(BF16), vn)) * (1.0 / l))
        o = outs[0] - lam * outs[1]
        o_ref[:, sl] = _rms(o, gs_ref[...]) * (1.0 - lam_init)


def _attn_sample(qb, kb, vb, kc, vc, lams, gs_row, *, n_heads, qk_dim, lam_init):
    bsz, length, w = qb.shape
    past = kc.shape[1]
    new = pl.BlockSpec((None, length, w), lambda b: (b, 0, 0))
    old = pl.BlockSpec((None, past, w), lambda b: (b, 0, 0))
    kern = functools.partial(_attn_sample_kernel, n_heads=n_heads, qk_dim=qk_dim, lam_init=lam_init)
    return pl.pallas_call(
        kern, grid=(bsz,),
        in_specs=[new, new, new, old, old] + [_full(a.shape) for a in lams] + [_full(gs_row.shape)],
        out_specs=new, out_shape=jax.ShapeDtypeStruct((bsz, length, w), F32),
        compiler_params=_cparams("parallel"),
    )(qb, kb, vb, kc, vc, *lams, gs_row)


def _mid_kernel(x_ref, ys_ref, oa_ref, woa_ref, wob_ref, gq_ref, wmq_ref, mk_ref, mv_ref, wmo_ref, x2_ref,
                *, n_heads, scale):
    x1 = x_ref[...] + _mm(ys_ref[...].astype(BF16), woa_ref[...]) + _mm(oa_ref[...].astype(BF16), wob_ref[...])
    qm = _mm(_rms(x1, gq_ref[...]).astype(BF16), wmq_ref[...])
    hd = qm.shape[1] // n_heads
    oms = []
    for h in range(n_heads):
        sl = slice(h * hd, (h + 1) * hd)
        s = _mm_nt(qm[:, sl].astype(BF16), mk_ref[:, sl].astype(BF16)) * scale
        p = jnp.exp(s - jnp.max(s, axis=1, keepdims=True))
        p = p * (1.0 / jnp.sum(p, axis=1, keepdims=True))
        oms.append(_mm(p.astype(BF16), mv_ref[:, sl].astype(BF16)))
    om = jnp.concatenate(oms, axis=1)
    x2_ref[...] = x1 + _mm(om.astype(BF16), wmo_ref[...])


def _mid(x, ys, oa, woa, wob, gq, wmq, mk, mv, wmo, *, tm, rows_per_batch, n_heads):
    n, d = x.shape
    tiles_per_batch = rows_per_batch // tm
    n_mem = mk.shape[1]
    row = lambda w: pl.BlockSpec((tm, w), lambda i: (i, 0))
    mem = pl.BlockSpec((None, n_mem, d), lambda i: (i // tiles_per_batch, 0, 0))
    kern = functools.partial(_mid_kernel, n_heads=n_heads, scale=1.0 / math.sqrt(d // n_heads))
    return pl.pallas_call(
        kern, grid=(n // tm,),
        in_specs=[row(d), row(ys.shape[1]), row(oa.shape[1]), _full(woa.shape), _full(wob.shape), _full(gq.shape),
                  _full(wmq.shape), mem, mem, _full(wmo.shape)],
        out_specs=row(d), out_shape=jax.ShapeDtypeStruct((n, d), F32),
        compiler_params=_cparams("parallel"),
    )(x, ys, oa, woa, wob, gq, wmq, mk, mv, wmo)


def _topk_rows(s, k):
    n, t = s.shape
    rows = lax.broadcasted_iota(I32, (n, t), 0)
    slot = lax.broadcasted_iota(I32, (k, t), 0)
    vals = jnp.zeros((k, t), F32)
    idxs = jnp.zeros((k, t), I32)
    for i in range(k):
        m = jnp.max(s, axis=0, keepdims=True)
        idx = jnp.min(jnp.where(s == m, rows, n), axis=0, keepdims=True)
        s = jnp.where(rows == idx, -jnp.inf, s)
        vals = jnp.where(slot == i, m, vals)
        idxs = jnp.where(slot == i, idx, idxs)
    return vals, idxs


def _pair_candidates(k):
    rows = [0 * k + j for j in range(k)]
    rows += [1 * k + j for j in range(8)]
    for i in range(2, 8):
        rows += [(i * k + j) if (i + 1) * (j + 1) <= k else -1 for j in range(8)]
    rows += [i * k for i in range(8, k)]
    return rows


def _peer_score_kernel(x2_ref, gf_ref, wpqt_ref, keys_ref, fid_ref, eid_ref, gate_ref, qt_ref, e_s, g_s,
                       *, n_heads, n_keys, topk):
    hf = _rms(x2_ref[...], gf_ref[...]).astype(BF16)
    qt_ref[...] = _mm_nt(wpqt_ref[...], hf).astype(BF16)
    half = wpqt_ref.shape[0] // (2 * n_heads)
    fid = fid_ref[...]
    tm = fid.shape[1]
    slot = lax.broadcasted_iota(I32, (topk, tm), 0)

    def head(h, carry):
        tops = []
        for c in range(2):
            r0 = pl.multiple_of((2 * h + c) * half, half)
            tops.append(_topk_rows(_mm(keys_ref[h, c], qt_ref[pl.ds(r0, half), :]), topk))
        (s1, i1), (s2, i2) = tops
        blocks_s = [s1[0:1] + s2, s1[1:2] + s2[0:8]]
        blocks_e = [i1[0:1] * n_keys + i2, i1[1:2] * n_keys + i2[0:8]]
        for i in range(2, 8):
            blocks_s.append(s1[i:i + 1] + s2[0:8])
            blocks_e.append(i1[i:i + 1] * n_keys + i2[0:8])
        blocks_s.append(s1[8:topk] + s2[0:1])
        blocks_e.append(i1[8:topk] * n_keys + i2[0:1])
        cand = jnp.where(fid >= 0, jnp.concatenate(blocks_s, axis=0), -jnp.inf)
        eid = jnp.concatenate(blocks_e, axis=0)
        sv = jnp.zeros((topk, tm), F32)
        ev = jnp.zeros((topk, tm), I32)
        for i in range(topk):
            m = jnp.max(cand, axis=0, keepdims=True)
            f = jnp.min(jnp.where(cand == m, fid, topk * topk), axis=0, keepdims=True)
            sel = fid == f
            e = jnp.max(jnp.where(sel, eid, -1), axis=0, keepdims=True)
            cand = jnp.where(sel, -jnp.inf, cand)
            sv = jnp.where(slot == i, m, sv)
            ev = jnp.where(slot == i, e, ev)
        p = jnp.exp(sv - sv[0:1])
        r0 = pl.multiple_of(h * topk, topk)
        e_s[pl.ds(r0, topk), :] = ev
        g_s[pl.ds(r0, topk), :] = p * (1.0 / jnp.sum(p, axis=0, keepdims=True))
        return carry

    lax.fori_loop(0, n_heads, head, 0)
    eid_ref[...] = e_s[...].T
    gate_ref[...] = g_s[...].T


def _peer_score(x2, gf, wpqt, keys, *, tm):
    n, d = x2.shape
    n_heads, _, n_keys, _ = keys.shape
    picks = n_heads * PEER_TOPK
    fid = jnp.tile(jnp.asarray(_pair_candidates(PEER_TOPK), I32)[:, None], (1, tm))
    kern = functools.partial(_peer_score_kernel, n_heads=n_heads, n_keys=n_keys, topk=PEER_TOPK)
    row = lambda w: pl.BlockSpec((tm, w), lambda i: (i, 0))
    return pl.pallas_call(
        kern, grid=(n // tm,),
        in_specs=[row(d), _full(gf.shape), _full(wpqt.shape), _full(keys.shape), _full(fid.shape)],
        out_specs=[row(picks), row(picks)],
        out_shape=[jax.ShapeDtypeStruct((n, picks), I32), jax.ShapeDtypeStruct((n, picks), F32)],
        scratch_shapes=[pltpu.VMEM((wpqt.shape[0], tm), BF16), pltpu.VMEM((picks, tm), I32),
                        pltpu.VMEM((picks, tm), F32)],
        compiler_params=_cparams("parallel"),
    )(x2, gf, wpqt, keys, fid)


def _gelu_tanh(x):
    return 0.5 * x * (1.0 + jnp.tanh(math.sqrt(2.0 / math.pi) * (x + 0.044715 * (x * x * x))))


def _peer_gather_kernel(eid_ref, eid_next_ref, gate_ref, x2_ref, gf_ref, gfin_ref, tab_ref, y_ref, buf0, buf1,
                        sems, *, tt, picks):
    d = x2_ref.shape[1]
    i = pl.program_id(0)
    bufs = (buf0, buf1)
    group = 2

    def issue(idx_ref, row, s, t):
        for k in range(picks):
            e = idx_ref[row, k]
            pltpu.make_async_copy(tab_ref.at[pl.ds(e, 1), :], bufs[s].at[t, pl.ds(k, 1), :], sems.at[s, t]).start()

    def wait(s, t):
        pltpu.make_async_copy(tab_ref.at[pl.ds(0, picks), :], bufs[s].at[t], sems.at[s, t]).wait()

    def compute(s, t, row):
        x2 = x2_ref[row:row + 1, :]
        hf = _rms(x2, gf_ref[...])
        rows = bufs[s][t]
        a = jnp.sum(rows[:, :d] * hf, axis=1, keepdims=True)
        act = _gelu_tanh(a) * gate_ref[:, row:row + 1]
        o = jnp.sum(rows[:, d:] * act, axis=0, keepdims=True)
        y_ref[row:row + 1, :] = _rms(x2 + o, gfin_ref[...])

    @pl.when(i == 0)
    def _():
        def token(t, carry):
            issue(eid_ref, t, 0, t)
            return carry

        lax.fori_loop(0, tt, token, 0)

    for s, idx_ref, idx_row0, out_row0 in ((0, eid_ref, tt, 0), (1, eid_next_ref, 0, tt)):
        for t0 in range(0, tt, group):
            for t in range(t0, t0 + group):
                wait(s, t)
            for t in range(t0, t0 + group):
                issue(idx_ref, idx_row0 + t, 1 - s, t)
            for t in range(t0, t0 + group):
                compute(s, t, out_row0 + t)

    @pl.when(i == pl.num_programs(0) - 1)
    def _():
        for t in range(tt):
            wait(0, t)


def _peer_gather(eid, gate_t, x2, gf, gfin, tab, *, tt):
    n, d = x2.shape
    picks = eid.shape[1]
    steps = n // (2 * tt)
    kern = functools.partial(_peer_gather_kernel, tt=tt, picks=picks)
    return pl.pallas_call(
        kern, grid=(steps,),
        in_specs=[pl.BlockSpec((2 * tt, picks), lambda i: (i, 0), memory_space=pltpu.SMEM),
                  pl.BlockSpec((tt, picks), lambda i: (jnp.minimum(2 * i + 2, 2 * steps - 1), 0),
                               memory_space=pltpu.SMEM),
                  pl.BlockSpec((None, picks, 2 * tt), lambda i: (i, 0, 0)),
                  pl.BlockSpec((2 * tt, d), lambda i: (i, 0)), _full(gf.shape), _full(gfin.shape),
                  pl.BlockSpec(memory_space=pl.ANY)],
        out_specs=pl.BlockSpec((2 * tt, d), lambda i: (i, 0)),
        out_shape=jax.ShapeDtypeStruct((n, d), F32),
        scratch_shapes=[pltpu.VMEM((tt, picks, tab.shape[1]), F32), pltpu.VMEM((tt, picks, tab.shape[1]), F32),
                        pltpu.SemaphoreType.DMA((2, tt))],
        compiler_params=_cparams("arbitrary"),
    )(eid, eid, gate_t, x2, gf, gfin, tab)


def _layer(x, pos_offset, k_past, v_past, h0, conv_prev, mem_k, mem_v, p, lam_init, dims, tiles):
    bsz, length, d = x.shape
    n = bsz * length
    nh, hd, ns, ng = dims["ssd_heads"], dims["ssd_head_dim"], dims["ssd_state"], dims["ssd_groups"]
    ah, qk = dims["diff_heads"], dims["qk_dim"]
    vd = 2 * qk
    x2d = x.reshape(n, d)

    tables = _rope_tables(length, pos_offset, qk)
    tm = tiles["tm"]
    if tm > length:
        tables = tuple(jnp.tile(t, (tm // length, 1)) for t in tables)
        cycle = tm
    else:
        cycle = length
    z, xbc, dt, qb, k, v, kb, vb = _in_proj(x2d, p["g_mix"], p["w_in_parts"], tables, tm, cycle, 1.0 / math.sqrt(qk))

    gw = nh * hd // ng
    h0_t = h0.reshape(bsz, ng, nh // ng, hd, ns).transpose(0, 1, 4, 2, 3).reshape(bsz, ng, ns, gw)
    cprev8 = jnp.pad(conv_prev, ((0, 0), (8 - (CONV_WIDTH - 1), 0), (0, 0)))
    q_len = min(CHUNK, length)
    y_ssd, h_last_t, conv_new = _ssd(xbc, dt, z, h0_t, cprev8, p["conv_w"], p["conv_b"], p["dt_bias_pad"],
                                     p["a_log_pad"], p["d_skip_x"], p["g_ssd"], p["expand"], bsz=bsz, length=length,
                                     q=q_len, tb=tiles["ssd_tb"], n_heads=nh, head_dim=hd, n_state=ns, n_groups=ng)
    h_last = h_last_t.reshape(bsz, ng, ns, nh // ng, hd).transpose(0, 1, 3, 4, 2).reshape(bsz, nh, hd, ns)

    lams = (p["lam_q1"], p["lam_k1"], p["lam_q2"], p["lam_k2"])
    if k_past is None:
        tq = tiles["attn_tq"]
        vt = vb.reshape(bsz, length // tq, tq, ah, vd).transpose(0, 3, 1, 4, 2)
        o = _attn_prompt(qb.reshape(bsz, length, ah * vd), kb.reshape(bsz, length, ah * vd), vt, lams,
                         p["g_subln"].reshape(vd, 1), tq=tq, n_heads=ah, qk_dim=qk, lam_init=lam_init)
    else:
        past = k_past.shape[1]
        o = _attn_sample(qb.reshape(bsz, length, ah * vd), kb.reshape(bsz, length, ah * vd),
                         vb.reshape(bsz, length, ah * vd), k_past.reshape(bsz, past, ah * vd),
                         v_past.reshape(bsz, past, ah * vd), lams, p["g_subln"].reshape(1, vd), n_heads=ah,
                         qk_dim=qk, lam_init=lam_init)

    n_mem = mem_k.shape[1]
    x2 = _mid(x2d, y_ssd, o.reshape(n, ah * vd), p["w_out_a"], p["w_out_b"], p["g_mem_q"], p["w_mq"],
              mem_k.reshape(bsz, n_mem, d), mem_v.reshape(bsz, n_mem, d), p["w_mo"], tm=tiles["mid_tm"],
              rows_per_batch=length, n_heads=dims["mem_heads"])

    eid, gate = _peer_score(x2, p["g_ffn"], p["w_pq_t"], p["peer_keys"], tm=tiles["peer_tm"])
    tt = tiles["gather_tt"]
    picks = eid.shape[1]
    gate_t = gate.reshape(n // (2 * tt), 2 * tt, picks).transpose(0, 2, 1)
    y = _peer_gather(eid, gate_t, x2, p["g_ffn"], p["g_final"], p["peer_tab"], tt=tt)
    return (y.reshape(bsz, length, d), k.reshape(bsz, length, ah, vd), v.reshape(bsz, length, ah, vd), h_last,
            conv_new)


def kernel(x_prompt, x_sample, cache_attn_k, cache_attn_v, cache_mem_k, cache_mem_v, state_ssm, state_conv, mem_prompt, g_mix, w_in, conv_w, conv_b, dt_bias, a_log, d_skip, g_ssd, lam_q1, lam_k1, lam_q2, lam_k2, g_subln, w_out, g_mem_q, g_mem_kv, w_mq, w_mk, w_mv, w_mo, g_ffn, w_pq, peer_keys, peer_u, peer_v, g_final):
    depth = w_in.shape[0]
    assert depth == 1, "the final norm is fused into the last layer's PEER kernel; one layer supported"
    bp, seq, d = x_prompt.shape
    _, _, nh, hd, ns = state_ssm.shape
    cdim = state_conv.shape[-1]
    width = nh * hd
    ng = (cdim - width) // (2 * ns)
    ah = cache_attn_k.shape[3]
    qk = lam_q1.shape[-1]
    dims = dict(ssd_heads=nh, ssd_head_dim=hd, ssd_state=ns, ssd_groups=ng, diff_heads=ah, qk_dim=qk,
                mem_heads=cache_mem_k.shape[3])
    sizes = (width, cdim, nh, ah * 2 * qk, ah * 2 * qk, ah * 2 * qk)
    splits = [sum(sizes[:i + 1]) for i in range(len(sizes) - 1)]
    n_mem = mem_prompt.shape[1]

    l = 0
    lam_init = 0.8 - 0.6 * math.exp(-0.3 * l)
    wz, wx, wdt, wq, wk, wv = jnp.split(w_in[l].astype(BF16), splits, axis=1)
    wdt = jnp.pad(wdt, ((0, 0), (0, LANES - nh)))
    pad_h = lambda a: jnp.pad(a.reshape(1, nh), ((0, 0), (0, LANES - nh)))
    expand = (jnp.arange(LANES)[:, None] == (jnp.arange(width) // hd)[None, :]).astype(F32)
    p = dict(
        g_mix=g_mix[l].reshape(1, d), w_in_parts=(wz, wx, wdt, wq, wk, wv),
        conv_w=conv_w[l], conv_b=conv_b[l].reshape(1, cdim), dt_bias_pad=pad_h(dt_bias[l]), a_log_pad=pad_h(a_log[l]),
        d_skip_x=jnp.repeat(d_skip[l], hd).reshape(1, width), g_ssd=g_ssd[l].reshape(1, width), expand=expand,
        lam_q1=lam_q1[l].reshape(1, qk), lam_k1=lam_k1[l].reshape(1, qk), lam_q2=lam_q2[l].reshape(1, qk),
        lam_k2=lam_k2[l].reshape(1, qk), g_subln=g_subln[l],
        w_out_a=w_out[l, :width].astype(BF16), w_out_b=w_out[l, width:].astype(BF16),
        g_mem_q=g_mem_q[l].reshape(1, d), w_mq=w_mq[l].astype(BF16), w_mo=w_mo[l].astype(BF16),
        g_ffn=g_ffn[l].reshape(1, d), w_pq_t=w_pq[l].T.astype(BF16), peer_keys=peer_keys[l].astype(BF16),
        peer_tab=jnp.concatenate([peer_u[l], peer_v[l]], axis=1), g_final=g_final.reshape(1, d),
    )

    mk_p, mv_p = _norm_mm2(mem_prompt.reshape(bp * n_mem, d), g_mem_kv[l].reshape(1, d), w_mk[l].astype(BF16),
                           w_mv[l].astype(BF16), tm=n_mem)
    mem_shape = (bp, n_mem) + cache_mem_k.shape[3:]
    mk_p, mv_p = mk_p.reshape(mem_shape), mv_p.reshape(mem_shape)

    tiles_p = dict(tm=min(512, seq), ssd_tb=min(512, seq), attn_tq=min(512, seq), mid_tm=min(512, seq),
                   peer_tm=256, gather_tt=8)
    h0_p = jnp.zeros((bp, nh, hd, ns), F32)
    conv0_p = jnp.zeros((bp, CONV_WIDTH - 1, cdim), F32)
    yp, kp, vp, hp, cp = _layer(x_prompt, 0, None, None, h0_p, conv0_p, mk_p, mv_p, p, lam_init, dims, tiles_p)

    bs, dseq, _ = x_sample.shape
    past = cache_attn_k.shape[2]
    ns_rows = bs * dseq
    tiles_s = dict(tm=min(512, ns_rows), ssd_tb=dseq, mid_tm=dseq, peer_tm=min(256, ns_rows), gather_tt=8)
    ys, ksn, vsn, hsn, csn = _layer(x_sample, past, cache_attn_k[l], cache_attn_v[l], state_ssm[l], state_conv[l],
                                    cache_mem_k[l], cache_mem_v[l], p, lam_init, dims, tiles_s)

    st = lambda a: a[None]
    return (yp, ys, st(kp), st(vp), st(hp), st(cp), st(mk_p), st(mv_p), st(ksn), st(vsn), st(hsn), st(csn))
```

```python
import functools
import math

import jax
import jax.numpy as jnp
from jax import lax
from jax.experimental import pallas as pl
from jax.experimental.pallas import tpu as pltpu

F32 = jnp.float32
BF16 = jnp.bfloat16
I32 = jnp.int32
EPS = 1e-6
LANES = 128
CHUNK = 64
ROT_HALF = 8
ROPE_THETA = 500000.0
CONV_WIDTH = 4
PEER_TOPK = 16
DMA_THREADS = 2
VMEM_LIMIT = 56 * 1024 * 1024
HIGHEST = lax.Precision.HIGHEST
NT_DIMS = (((1,), (1,)), ((), ()))


def _cparams(*sem):
    return pltpu.CompilerParams(dimension_semantics=sem, vmem_limit_bytes=VMEM_LIMIT)


def _rms(x, g):
    return x * lax.rsqrt(jnp.mean(x * x, axis=-1, keepdims=True) + EPS) * g


def _mm(a, b):
    return jnp.dot(a, b, preferred_element_type=F32)


def _mm_nt(a, b):
    return lax.dot_general(a, b, NT_DIMS, preferred_element_type=F32)


def _full(shape):
    return pl.BlockSpec(shape, lambda *_: (0,) * len(shape))


def _in_proj_kernel(x_ref, g_ref, wz_ref, wx_ref, wdt_ref, wq_ref, wk_ref, wv_ref, c_ref, s1_ref, s2_ref,
                    z_ref, xbc_ref, dt_ref, q_ref, k_ref, v_ref, kb_ref, vb_ref, *, q_scale):
    h = _rms(x_ref[...], g_ref[...]).astype(BF16)
    z_ref[...] = _mm(h, wz_ref[...])
    xbc_ref[...] = _mm(h, wx_ref[...])
    dt_ref[...] = _mm(h, wdt_ref[...])
    c, s1, s2 = c_ref[...], s1_ref[...], s2_ref[...]

    def rope(t):
        parts = []
        for gi in range(t.shape[1] // LANES):
            tg = t[:, gi * LANES:(gi + 1) * LANES]
            parts.append(tg * c + pltpu.roll(tg, ROT_HALF, 1) * s1 + pltpu.roll(tg, LANES - ROT_HALF, 1) * s2)
        return jnp.concatenate(parts, axis=1)

    q_ref[...] = (rope(_mm(h, wq_ref[...])) * q_scale).astype(BF16)
    k = rope(_mm(h, wk_ref[...]))
    k_ref[...] = k
    kb_ref[...] = k.astype(BF16)
    v = _mm(h, wv_ref[...])
    v_ref[...] = v
    vb_ref[...] = v.astype(BF16)


def _rope_tables(length, offset, qk_dim):
    inv = 1.0 / (ROPE_THETA ** (jnp.arange(ROT_HALF, dtype=F32) / ROT_HALF))
    pos = jnp.arange(length, dtype=F32) + offset
    ang = pos[:, None] * inv[None, :]
    cos, sin = jnp.cos(ang), jnp.sin(ang)
    rest = qk_dim - 2 * ROT_HALF
    one, zero = jnp.ones((length, rest), F32), jnp.zeros((length, rest), F32)
    z8 = jnp.zeros((length, ROT_HALF), F32)
    rep = LANES // qk_dim
    c = jnp.tile(jnp.concatenate([cos, cos, one], axis=1), (1, rep))
    s1 = jnp.tile(jnp.concatenate([z8, sin, zero], axis=1), (1, rep))
    s2 = jnp.tile(jnp.concatenate([-sin, z8, zero], axis=1), (1, rep))
    return c, s1, s2


def _in_proj(x, g, ws, tables, tm, rows_per_table_cycle, q_scale):
    n, d = x.shape
    wz, wx, wdt, wq, wk, wv = ws
    nt = rows_per_table_cycle // tm
    row = lambda w: pl.BlockSpec((tm, w), lambda i: (i, 0))
    tab = pl.BlockSpec((tm, LANES), lambda i: (i % nt, 0))
    widths = [wz.shape[1], wx.shape[1], wdt.shape[1], wq.shape[1], wk.shape[1], wv.shape[1]]
    out_shape = [jax.ShapeDtypeStruct((n, w), F32) for w in widths[:3]]
    out_shape += [jax.ShapeDtypeStruct((n, widths[3]), BF16), jax.ShapeDtypeStruct((n, widths[4]), F32),
                  jax.ShapeDtypeStruct((n, widths[5]), F32), jax.ShapeDtypeStruct((n, widths[4]), BF16),
                  jax.ShapeDtypeStruct((n, widths[5]), BF16)]
    out_specs = [row(w) for w in widths[:3]] + [row(widths[3]), row(widths[4]), row(widths[5]), row(widths[4]),
                                                 row(widths[5])]
    return pl.pallas_call(
        functools.partial(_in_proj_kernel, q_scale=q_scale),
        grid=(n // tm,),
        in_specs=[row(d), _full((1, d))] + [_full(w.shape) for w in ws] + [tab, tab, tab],
        out_specs=out_specs, out_shape=out_shape,
        compiler_params=_cparams("parallel"),
    )(x, g, *ws, *tables)


def _norm_mm2_kernel(x_ref, g_ref, wa_ref, wb_ref, a_ref, b_ref):
    h = _rms(x_ref[...], g_ref[...]).astype(BF16)
    a_ref[...] = _mm(h, wa_ref[...])
    b_ref[...] = _mm(h, wb_ref[...])


def _norm_mm2(x, g, wa, wb, tm):
    n, d = x.shape
    row = lambda w: pl.BlockSpec((tm, w), lambda i: (i, 0))
    return pl.pallas_call(
        _norm_mm2_kernel, grid=(n // tm,),
        in_specs=[row(d), _full((1, d)), _full(wa.shape), _full(wb.shape)],
        out_specs=[row(wa.shape[1]), row(wb.shape[1])],
        out_shape=[jax.ShapeDtypeStruct((n, wa.shape[1]), F32), jax.ShapeDtypeStruct((n, wb.shape[1]), F32)],
        compiler_params=_cparams("parallel"),
    )(x, g, wa, wb)


def _pad_rows_to_lanes(x):
    q = x.shape[0]
    if q == LANES:
        return x
    return jnp.concatenate([x, jnp.zeros((LANES - q, x.shape[1]), x.dtype)], axis=0)


def _ssd_kernel(xbc_ref, dt_ref, z_ref, h0_ref, cprev_ref, cw_ref, cb_ref, dtb_ref, alog_ref, dskip_ref, gssd_ref,
                expand_ref, y_ref, hl_ref, cn_ref, xp_ref, xc_ref, s_ref, *, q, tb, n_heads, head_dim, n_state,
                n_groups):
    t = pl.program_id(1)
    width = n_heads * head_dim
    gw = width // n_groups
    hpg = n_heads // n_groups
    halo = 8

    @pl.when(t == 0)
    def _():
        s_ref[...] = h0_ref[0]
        xp_ref[0:halo, :] = cprev_ref[0]

    xp_ref[halo:halo + tb, :] = xbc_ref[...]
    cw = cw_ref[...]
    conv = cb_ref[...]
    for j in range(CONV_WIDTH):
        off = halo - (CONV_WIDTH - 1) + j
        conv = conv + cw[j:j + 1, :] * xp_ref[off:off + tb, :]
    xc_ref[...] = conv * jax.nn.sigmoid(conv)
    cn_ref[0] = xp_ref[halo + tb - (CONV_WIDTH - 1):halo + tb, :]
    xp_ref[0:halo, :] = xp_ref[tb:tb + halo, :]

    a_neg = -jnp.exp(alog_ref[...])
    ri = lax.broadcasted_iota(I32, (q, q), 0)
    ci = lax.broadcasted_iota(I32, (q, q), 1)
    tril = ci <= ri
    tril_f = tril.astype(F32)
    lane_g = lax.broadcasted_iota(I32, (1, gw), 1)

    def chunk(c, carry):
        r0 = pl.multiple_of(c * q, q)
        xc = xc_ref[pl.ds(r0, q), :]
        xs = xc[:, :width]
        bm = xc[:, width:width + n_groups * n_state]
        cm = xc[:, width + n_groups * n_state:]
        dtr = dt_ref[pl.ds(r0, q), :] + dtb_ref[...]
        dt = jnp.maximum(dtr, 0.0) + jnp.log1p(jnp.exp(-jnp.abs(dtr)))
        da = dt * a_neg
        a_cum = jnp.dot(tril_f, da, precision=HIGHEST, preferred_element_type=F32)
        a_last = a_cum[q - 1:q, :]
        stk = jnp.concatenate([dt, jnp.exp(a_last - a_cum), jnp.exp(a_cum)], axis=0)
        ex = jnp.dot(stk, expand_ref[...], precision=HIGHEST, preferred_element_type=F32)
        dt_x, te_x, ec_x = ex[:q], ex[q:2 * q], ex[2 * q:]
        cd_x = ec_x[q - 1:q, :]
        xdt = xs * dt_x
        wgt = xdt * te_x
        a_cum_t = _pad_rows_to_lanes(a_cum).T
        ys = []
        for g in range(n_groups):
            bg = bm[:, g * n_state:(g + 1) * n_state]
            cg = cm[:, g * n_state:(g + 1) * n_state].astype(BF16)
            bg_t = _pad_rows_to_lanes(bg).T[:, :q].astype(BF16)
            cb = _mm_nt(cg, bg.astype(BF16))
            s_in = s_ref[g]
            gs = slice(g * gw, (g + 1) * gw)
            y_g = _mm(cg, s_in.astype(BF16)) * ec_x[:, gs] + dskip_ref[:, gs] * xs[:, gs]
            xdt_g = xdt[:, gs]
            for hl in range(hpg):
                h = g * hpg + hl
                seg = a_cum[:, h:h + 1] - a_cum_t[h:h + 1, :q]
                m_h = (cb * jnp.exp(jnp.where(tril, seg, -jnp.inf))).astype(BF16)
                in_head = (lane_g >= hl * head_dim) & (lane_g < (hl + 1) * head_dim)
                y_g = y_g + _mm(m_h, jnp.where(in_head, xdt_g, 0.0).astype(BF16))
            s_ref[g] = s_in * cd_x[:, gs] + _mm(bg_t, wgt[:, gs].astype(BF16))
            ys.append(y_g)
        y = jnp.concatenate(ys, axis=1)
        zc = z_ref[pl.ds(r0, q), :]
        y_ref[pl.ds(r0, q), :] = _rms(y * (zc * jax.nn.sigmoid(zc)), gssd_ref[...])
        return carry

    lax.fori_loop(0, tb // q, chunk, 0)

    @pl.when(t == pl.num_programs(1) - 1)
    def _():
        hl_ref[0] = s_ref[...]


def _ssd(xbc, dt, z, h0_t, cprev8, cw, cb, dtb, alog, dskip_x, gssd, expand, *, bsz, length, q, tb, n_heads,
         head_dim, n_state, n_groups):
    nt = length // tb
    width = n_heads * head_dim
    cdim = xbc.shape[1]
    row = lambda w: pl.BlockSpec((tb, w), lambda b, t: (b * nt + t, 0))
    gw = width // n_groups
    kern = functools.partial(_ssd_kernel, q=q, tb=tb, n_heads=n_heads, head_dim=head_dim, n_state=n_state,
                             n_groups=n_groups)
    return pl.pallas_call(
        kern, grid=(bsz, nt),
        in_specs=[row(cdim), row(LANES), row(width),
                  pl.BlockSpec((1, n_groups, n_state, gw), lambda b, t: (b, 0, 0, 0)),
                  pl.BlockSpec((1, 8, cdim), lambda b, t: (b, 0, 0)),
                  _full(cw.shape), _full(cb.shape), _full(dtb.shape), _full(alog.shape), _full(dskip_x.shape),
                  _full(gssd.shape), _full(expand.shape)],
        out_specs=[row(width),
                   pl.BlockSpec((1, n_groups, n_state, gw), lambda b, t: (b, 0, 0, 0)),
                   pl.BlockSpec((1, CONV_WIDTH - 1, cdim), lambda b, t: (b, 0, 0))],
        out_shape=[jax.ShapeDtypeStruct((bsz * length, width), F32),
                   jax.ShapeDtypeStruct((bsz, n_groups, n_state, gw), F32),
                   jax.ShapeDtypeStruct((bsz, CONV_WIDTH - 1, cdim), F32)],
        scratch_shapes=[pltpu.VMEM((tb + 8, cdim), F32), pltpu.VMEM((tb, cdim), F32),
                        pltpu.VMEM((n_groups, n_state, gw), F32)],
        compiler_params=_cparams("parallel", "arbitrary"),
    )(xbc, dt, z, h0_t, cprev8, cw, cb, dtb, alog, dskip_x, gssd, expand)


def _lambda(lq1_ref, lk1_ref, lq2_ref, lk2_ref, lam_init):
    e1 = jnp.exp(jnp.sum(lq1_ref[...] * lk1_ref[...], axis=1, keepdims=True))
    e2 = jnp.exp(jnp.sum(lq2_ref[...] * lk2_ref[...], axis=1, keepdims=True))
    return e1 - e2 + lam_init


def _split_components(qh, qk_dim):
    lane = lax.broadcasted_iota(I32, qh.shape, 1)
    zero = jnp.zeros_like(qh)
    return jnp.where(lane < qk_dim, qh, zero), jnp.where(lane >= qk_dim, qh, zero)


def _attn_prompt_kernel(q_ref, k_ref, vt_ref, lq1_ref, lk1_ref, lq2_ref, lk2_ref, gs_ref, o_ref,
                        acc1, acc2, m1, l1, m2, l2, s1_ref, s2_ref, *, tq, qk_dim, lam_init):
    qi = pl.program_id(2)
    qq1, qq2 = _split_components(q_ref[...], qk_dim)
    for acc, m, l in ((acc1, m1, l1), (acc2, m2, l2)):
        acc[...] = jnp.zeros_like(acc)
        m[...] = jnp.full_like(m, -jnp.inf)
        l[...] = jnp.zeros_like(l)

    def scores(j):
        kk = k_ref[pl.ds(pl.multiple_of(j * tq, tq), tq), :]
        return _mm_nt(kk, qq1), _mm_nt(kk, qq2)

    def consume(j, s_pair):
        vt = vt_ref[j]
        for s, acc, m, l in ((s_pair[0], acc1, m1, l1), (s_pair[1], acc2, m2, l2)):
            m_old = m[...]
            m_new = jnp.maximum(m_old, jnp.max(s, axis=0, keepdims=True))
            alpha = jnp.exp(m_old - m_new)
            p = jnp.exp(s - m_new)
            l[...] = alpha * l[...] + jnp.sum(p, axis=0, keepdims=True)
            acc[...] = alpha * acc[...] + _mm(vt, p.astype(BF16))
            m[...] = m_new

    s1_ref[...], s2_ref[...] = scores(0)

    def body(j, carry):
        cur = (s1_ref[...], s2_ref[...])
        nxt = scores(j + 1)
        consume(j, cur)
        s1_ref[...], s2_ref[...] = nxt
        return carry

    lax.fori_loop(0, qi, body, 0)
    kr = lax.broadcasted_iota(I32, (tq, tq), 0) // CHUNK
    qc = lax.broadcasted_iota(I32, (tq, tq), 1) // CHUNK
    visible = kr <= qc
    consume(qi, (jnp.where(visible, s1_ref[...], -jnp.inf), jnp.where(visible, s2_ref[...], -jnp.inf)))

    lam = _lambda(lq1_ref, lk1_ref, lq2_ref, lk2_ref, lam_init)
    o = acc1[...] * (1.0 / l1[...]) - lam * (acc2[...] * (1.0 / l2[...]))
    r = o * lax.rsqrt(jnp.mean(o * o, axis=0, keepdims=True) + EPS) * gs_ref[...]
    o_ref[...] = (r * (1.0 - lam_init)).T


def _attn_prompt(qb, kb, vt, lams, gs_col, *, tq, n_heads, qk_dim, lam_init):
    bsz, s, _ = qb.shape
    vd = vt.shape[3]
    nblk = s // tq
    kern = functools.partial(_attn_prompt_kernel, tq=tq, qk_dim=qk_dim, lam_init=lam_init)
    return pl.pallas_call(
        kern, grid=(bsz, n_heads, nblk),
        in_specs=[pl.BlockSpec((None, tq, 2 * qk_dim), lambda b, h, i: (b, i, h)),
                  pl.BlockSpec((None, s, 2 * qk_dim), lambda b, h, i: (b, 0, h)),
                  pl.BlockSpec((None, None, nblk, vd, tq), lambda b, h, i: (b, h, 0, 0, 0))]
                 + [_full(a.shape) for a in lams] + [_full(gs_col.shape)],
        out_specs=pl.BlockSpec((None, tq, vd), lambda b, h, i: (b, i, h)),
        out_shape=jax.ShapeDtypeStruct((bsz, s, n_heads * vd), F32),
        scratch_shapes=[pltpu.VMEM((vd, tq), F32), pltpu.VMEM((vd, tq), F32)] + [pltpu.VMEM((1, tq), F32)] * 4
                       + [pltpu.VMEM((tq, tq), F32)] * 2,
        compiler_params=_cparams("parallel", "parallel", "arbitrary"),
    )(qb, kb, vt, *lams, gs_col)


def _attn_sample_kernel(q_ref, kn_ref, vn_ref, kc_ref, vc_ref, lq1_ref, lk1_ref, lq2_ref, lk2_ref, gs_ref, o_ref,
                        *, n_heads, qk_dim, lam_init):
    lam = _lambda(lq1_ref, lk1_ref, lq2_ref, lk2_ref, lam_init)
    vd = 2 * qk_dim
    for h in range(n_heads):
        sl = slice(h * vd, (h + 1) * vd)
        qq = _split_components(q_ref[:, sl], qk_dim)
        kp = kc_ref[:, sl].astype(BF16)
        vp = vc_ref[:, sl].astype(BF16)
        kn = kn_ref[:, sl]
        vn = vn_ref[:, sl]
        outs = []
        for qc in qq:
            sp = _mm_nt(qc, kp)
            sn = _mm_nt(qc, kn)
            m = jnp.maximum(jnp.max(sp, axis=1, keepdims=True), jnp.max(sn, axis=1, keepdims=True))
            pp = jnp.exp(sp - m)
            pn = jnp.exp(sn - m)
            l = jnp.sum(pp, axis=1, keepdims=True) + jnp.sum(pn, axis=1, keepdims=True)
            outs.append((_mm(pp.astype(BF16), vp) + _mm(pn.astype(BF16), vn)) * (1.0 / l))
        o = outs[0] - lam * outs[1]
        o_ref[:, sl] = _rms(o, gs_ref[...]) * (1.0 - lam_init)


def _attn_sample(qb, kb, vb, kc, vc, lams, gs_row, *, n_heads, qk_dim, lam_init):
    bsz, length, w = qb.shape
    past = kc.shape[1]
    new = pl.BlockSpec((None, length, w), lambda b: (b, 0, 0))
    old = pl.BlockSpec((None, past, w), lambda b: (b, 0, 0))
    kern = functools.partial(_attn_sample_kernel, n_heads=n_heads, qk_dim=qk_dim, lam_init=lam_init)
    return pl.pallas_call(
        kern, grid=(bsz,),
        in_specs=[new, new, new, old, old] + [_full(a.shape) for a in lams] + [_full(gs_row.shape)],
        out_specs=new, out_shape=jax.ShapeDtypeStruct((bsz, length, w), F32),
        compiler_params=_cparams("parallel"),
    )(qb, kb, vb, kc, vc, *lams, gs_row)


def _mid_kernel(x_ref, ys_ref, oa_ref, woa_ref, wob_ref, gq_ref, wmq_ref, mk_ref, mv_ref, wmo_ref, x2_ref,
                *, n_heads, scale):
    x1 = x_ref[...] + _mm(ys_ref[...].astype(BF16), woa_ref[...]) + _mm(oa_ref[...].astype(BF16), wob_ref[...])
    qm = _mm(_rms(x1, gq_ref[...]).astype(BF16), wmq_ref[...])
    hd = qm.shape[1] // n_heads
    oms = []
    for h in range(n_heads):
        sl = slice(h * hd, (h + 1) * hd)
        s = _mm_nt(qm[:, sl].astype(BF16), mk_ref[:, sl].astype(BF16)) * scale
        p = jnp.exp(s - jnp.max(s, axis=1, keepdims=True))
        p = p * (1.0 / jnp.sum(p, axis=1, keepdims=True))
        oms.append(_mm(p.astype(BF16), mv_ref[:, sl].astype(BF16)))
    om = jnp.concatenate(oms, axis=1)
    x2_ref[...] = x1 + _mm(om.astype(BF16), wmo_ref[...])


def _mid(x, ys, oa, woa, wob, gq, wmq, mk, mv, wmo, *, tm, rows_per_batch, n_heads):
    n, d = x.shape
    tiles_per_batch = rows_per_batch // tm
    n_mem = mk.shape[1]
    row = lambda w: pl.BlockSpec((tm, w), lambda i: (i, 0))
    mem = pl.BlockSpec((None, n_mem, d), lambda i: (i // tiles_per_batch, 0, 0))
    kern = functools.partial(_mid_kernel, n_heads=n_heads, scale=1.0 / math.sqrt(d // n_heads))
    return pl.pallas_call(
        kern, grid=(n // tm,),
        in_specs=[row(d), row(ys.shape[1]), row(oa.shape[1]), _full(woa.shape), _full(wob.shape), _full(gq.shape),
                  _full(wmq.shape), mem, mem, _full(wmo.shape)],
        out_specs=row(d), out_shape=jax.ShapeDtypeStruct((n, d), F32),
        compiler_params=_cparams("parallel"),
    )(x, ys, oa, woa, wob, gq, wmq, mk, mv, wmo)


def _topk_rows(s, k):
    n, t = s.shape
    rows = lax.broadcasted_iota(I32, (n, t), 0)
    slot = lax.broadcasted_iota(I32, (k, t), 0)
    vals = jnp.zeros((k, t), F32)
    idxs = jnp.zeros((k, t), I32)
    for i in range(k):
        m = jnp.max(s, axis=0, keepdims=True)
        idx = jnp.min(jnp.where(s == m, rows, n), axis=0, keepdims=True)
        s = jnp.where(rows == idx, -jnp.inf, s)
        vals = jnp.where(slot == i, m, vals)
        idxs = jnp.where(slot == i, idx, idxs)
    return vals, idxs


def _pair_candidates(k):
    rows = [0 * k + j for j in range(k)]
    rows += [1 * k + j for j in range(8)]
    for i in range(2, 8):
        rows += [(i * k + j) if (i + 1) * (j + 1) <= k else -1 for j in range(8)]
    rows += [i * k for i in range(8, k)]
    return rows


def _peer_score_kernel(x2_ref, gf_ref, wpqt_ref, keys_ref, fid_ref, eid_ref, gate_ref, qt_ref, e_s, g_s,
                       *, n_heads, n_keys, topk):
    hf = _rms(x2_ref[...], gf_ref[...]).astype(BF16)
    qt_ref[...] = _mm_nt(wpqt_ref[...], hf).astype(BF16)
    half = wpqt_ref.shape[0] // (2 * n_heads)
    fid = fid_ref[...]
    tm = fid.shape[1]
    slot = lax.broadcasted_iota(I32, (topk, tm), 0)

    def head(h, carry):
        tops = []
        for c in range(2):
            r0 = pl.multiple_of((2 * h + c) * half, half)
            tops.append(_topk_rows(_mm(keys_ref[h, c], qt_ref[pl.ds(r0, half), :]), topk))
        (s1, i1), (s2, i2) = tops
        blocks_s = [s1[0:1] + s2, s1[1:2] + s2[0:8]]
        blocks_e = [i1[0:1] * n_keys + i2, i1[1:2] * n_keys + i2[0:8]]
        for i in range(2, 8):
            blocks_s.append(s1[i:i + 1] + s2[0:8])
            blocks_e.append(i1[i:i + 1] * n_keys + i2[0:8])
        blocks_s.append(s1[8:topk] + s2[0:1])
        blocks_e.append(i1[8:topk] * n_keys + i2[0:1])
        cand = jnp.where(fid >= 0, jnp.concatenate(blocks_s, axis=0), -jnp.inf)
        eid = jnp.concatenate(blocks_e, axis=0)
        sv = jnp.zeros((topk, tm), F32)
        ev = jnp.zeros((topk, tm), I32)
        for i in range(topk):
            m = jnp.max(cand, axis=0, keepdims=True)
            f = jnp.min(jnp.where(cand == m, fid, topk * topk), axis=0, keepdims=True)
            sel = fid == f
            e = jnp.max(jnp.where(sel, eid, -1), axis=0, keepdims=True)
            cand = jnp.where(sel, -jnp.inf, cand)
            sv = jnp.where(slot == i, m, sv)
            ev = jnp.where(slot == i, e, ev)
        p = jnp.exp(sv - sv[0:1])
        r0 = pl.multiple_of(h * topk, topk)
        e_s[pl.ds(r0, topk), :] = ev
        g_s[pl.ds(r0, topk), :] = p * (1.0 / jnp.sum(p, axis=0, keepdims=True))
        return carry

    lax.fori_loop(0, n_heads, head, 0)
    eid_ref[...] = e_s[...].T
    gate_ref[...] = g_s[...].T


def _peer_score(x2, gf, wpqt, keys, *, tm):
    n, d = x2.shape
    n_heads, _, n_keys, _ = keys.shape
    picks = n_heads * PEER_TOPK
    fid = jnp.tile(jnp.asarray(_pair_candidates(PEER_TOPK), I32)[:, None], (1, tm))
    kern = functools.partial(_peer_score_kernel, n_heads=n_heads, n_keys=n_keys, topk=PEER_TOPK)
    row = lambda w: pl.BlockSpec((tm, w), lambda i: (i, 0))
    return pl.pallas_call(
        kern, grid=(n // tm,),
        in_specs=[row(d), _full(gf.shape), _full(wpqt.shape), _full(keys.shape), _full(fid.shape)],
        out_specs=[row(picks), row(picks)],
        out_shape=[jax.ShapeDtypeStruct((n, picks), I32), jax.ShapeDtypeStruct((n, picks), F32)],
        scratch_shapes=[pltpu.VMEM((wpqt.shape[0], tm), BF16), pltpu.VMEM((picks, tm), I32),
                        pltpu.VMEM((picks, tm), F32)],
        compiler_params=_cparams("parallel"),
    )(x2, gf, wpqt, keys, fid)


def _gelu_tanh(x):
    return 0.5 * x * (1.0 + jnp.tanh(math.sqrt(2.0 / math.pi) * (x + 0.044715 * (x * x * x))))


def _peer_gather_kernel(eid_ref, eid_next_ref, gate_ref, x2_ref, gf_ref, gfin_ref, tab_ref, y_ref, buf0, buf1,
                        sems, *, tt, picks):
    dc = x2_ref.shape[1]
    d = dc * LANES
    i = pl.program_id(0)
    bufs = (buf0, buf1)
    group = 2

    def issue(idx_ref, row, s, t):
        for k in range(picks):
            e = idx_ref[row, k]
            pltpu.make_async_copy(tab_ref.at[e], bufs[s].at[t, k // 8, :, k % 8, :], sems.at[s, t]).start(
                priority=k % DMA_THREADS)

    def wait(s, t):
        pltpu.make_async_copy(bufs[1 - s].at[t], bufs[s].at[t], sems.at[s, t]).wait()

    def compute(s, t, row):
        x2 = x2_ref[row]
        ms = jnp.sum(jnp.sum(x2 * x2, axis=1, keepdims=True), axis=0, keepdims=True) * (1.0 / d)
        hf = x2 * lax.rsqrt(ms + EPS) * gf_ref[...]
        rows = bufs[s][t]
        pu = rows[:, 0] * hf[0:1, :]
        for c in range(1, dc):
            pu = pu + rows[:, c] * hf[c:c + 1, :]
        a = jnp.sum(pu, axis=2, keepdims=True)
        act = jnp.broadcast_to(_gelu_tanh(a) * gate_ref[:, :, row:row + 1], pu.shape)
        x3 = []
        for c in range(dc):
            oc = jnp.sum(jnp.sum(rows[:, dc + c] * act, axis=0), axis=0, keepdims=True)
            x3.append(x2[c:c + 1, :] + oc)
        ms3 = x3[0] * x3[0]
        for c in range(1, dc):
            ms3 = ms3 + x3[c] * x3[c]
        scale = lax.rsqrt(jnp.sum(ms3, axis=1, keepdims=True) * (1.0 / d) + EPS)
        for c in range(dc):
            y_ref[row, c:c + 1, :] = x3[c] * scale * gfin_ref[c:c + 1, :]

    @pl.when(i == 0)
    def _():
        def token(t, carry):
            issue(eid_ref, t, 0, t)
            return carry

        lax.fori_loop(0, tt, token, 0)

    for s, idx_ref, idx_row0, out_row0 in ((0, eid_ref, tt, 0), (1, eid_next_ref, 0, tt)):
        for t0 in range(0, tt, group):
            for t in range(t0, t0 + group):
                wait(s, t)
            for t in range(t0, t0 + group):
                issue(idx_ref, idx_row0 + t, 1 - s, t)
            for t in range(t0, t0 + group):
                compute(s, t, out_row0 + t)

    @pl.when(i == pl.num_programs(0) - 1)
    def _():
        for t in range(tt):
            wait(0, t)


def _peer_gather(eid, gate, x2, gf, gfin, tab, *, tt):
    n, d = x2.shape
    picks = eid.shape[1]
    dc = d // LANES
    steps = n // (2 * tt)
    tab3 = tab.reshape(tab.shape[0], tab.shape[1] // LANES, LANES)
    gate4 = gate.reshape(steps, 2 * tt, picks // 8, 8).transpose(0, 2, 3, 1)
    kern = functools.partial(_peer_gather_kernel, tt=tt, picks=picks)
    bshape = (tt, picks // 8, 2 * dc, 8, LANES)
    y = pl.pallas_call(
        kern, grid=(steps,),
        in_specs=[pl.BlockSpec((2 * tt, picks), lambda i: (i, 0), memory_space=pltpu.SMEM),
                  pl.BlockSpec((tt, picks), lambda i: (jnp.minimum(2 * i + 2, 2 * steps - 1), 0),
                               memory_space=pltpu.SMEM),
                  pl.BlockSpec((None, picks // 8, 8, 2 * tt), lambda i: (i, 0, 0, 0)),
                  pl.BlockSpec((2 * tt, dc, LANES), lambda i: (i, 0, 0)), _full((dc, LANES)), _full((dc, LANES)),
                  pl.BlockSpec(memory_space=pl.ANY)],
        out_specs=pl.BlockSpec((2 * tt, dc, LANES), lambda i: (i, 0, 0)),
        out_shape=jax.ShapeDtypeStruct((n, dc, LANES), F32),
        scratch_shapes=[pltpu.VMEM(bshape, F32), pltpu.VMEM(bshape, F32), pltpu.SemaphoreType.DMA((2, tt))],
        compiler_params=_cparams("arbitrary"),
    )(eid, eid, gate4, x2.reshape(n, dc, LANES), gf.reshape(dc, LANES), gfin.reshape(dc, LANES), tab3)
    return y.reshape(n, d)


def _layer(x, pos_offset, k_past, v_past, h0, conv_prev, mem_k, mem_v, p, lam_init, dims, tiles):
    bsz, length, d = x.shape
    n = bsz * length
    nh, hd, ns, ng = dims["ssd_heads"], dims["ssd_head_dim"], dims["ssd_state"], dims["ssd_groups"]
    ah, qk = dims["diff_heads"], dims["qk_dim"]
    vd = 2 * qk
    x2d = x.reshape(n, d)

    tables = _rope_tables(length, pos_offset, qk)
    tm = tiles["tm"]
    if tm > length:
        tables = tuple(jnp.tile(t, (tm // length, 1)) for t in tables)
        cycle = tm
    else:
        cycle = length
    z, xbc, dt, qb, k, v, kb, vb = _in_proj(x2d, p["g_mix"], p["w_in_parts"], tables, tm, cycle, 1.0 / math.sqrt(qk))

    gw = nh * hd // ng
    h0_t = h0.reshape(bsz, ng, nh // ng, hd, ns).transpose(0, 1, 4, 2, 3).reshape(bsz, ng, ns, gw)
    cprev8 = jnp.pad(conv_prev, ((0, 0), (8 - (CONV_WIDTH - 1), 0), (0, 0)))
    q_len = min(CHUNK, length)
    y_ssd, h_last_t, conv_new = _ssd(xbc, dt, z, h0_t, cprev8, p["conv_w"], p["conv_b"], p["dt_bias_pad"],
                                     p["a_log_pad"], p["d_skip_x"], p["g_ssd"], p["expand"], bsz=bsz, length=length,
                                     q=q_len, tb=tiles["ssd_tb"], n_heads=nh, head_dim=hd, n_state=ns, n_groups=ng)
    h_last = h_last_t.reshape(bsz, ng, ns, nh // ng, hd).transpose(0, 1, 3, 4, 2).reshape(bsz, nh, hd, ns)

    lams = (p["lam_q1"], p["lam_k1"], p["lam_q2"], p["lam_k2"])
    if k_past is None:
        tq = tiles["attn_tq"]
        vt = vb.reshape(bsz, length // tq, tq, ah, vd).transpose(0, 3, 1, 4, 2)
        o = _attn_prompt(qb.reshape(bsz, length, ah * vd), kb.reshape(bsz, length, ah * vd), vt, lams,
                         p["g_subln"].reshape(vd, 1), tq=tq, n_heads=ah, qk_dim=qk, lam_init=lam_init)
    else:
        past = k_past.shape[1]
        o = _attn_sample(qb.reshape(bsz, length, ah * vd), kb.reshape(bsz, length, ah * vd),
                         vb.reshape(bsz, length, ah * vd), k_past.reshape(bsz, past, ah * vd),
                         v_past.reshape(bsz, past, ah * vd), lams, p["g_subln"].reshape(1, vd), n_heads=ah,
                         qk_dim=qk, lam_init=lam_init)

    n_mem = mem_k.shape[1]
    x2 = _mid(x2d, y_ssd, o.reshape(n, ah * vd), p["w_out_a"], p["w_out_b"], p["g_mem_q"], p["w_mq"],
              mem_k.reshape(bsz, n_mem, d), mem_v.reshape(bsz, n_mem, d), p["w_mo"], tm=tiles["mid_tm"],
              rows_per_batch=length, n_heads=dims["mem_heads"])

    eid, gate = _peer_score(x2, p["g_ffn"], p["w_pq_t"], p["peer_keys"], tm=tiles["peer_tm"])
    tt = tiles["gather_tt"]
    y = _peer_gather(eid, gate, x2, p["g_ffn"], p["g_final"], p["peer_tab"], tt=tt)
    return (y.reshape(bsz, length, d), k.reshape(bsz, length, ah, vd), v.reshape(bsz, length, ah, vd), h_last,
            conv_new)


def kernel(x_prompt, x_sample, cache_attn_k, cache_attn_v, cache_mem_k, cache_mem_v, state_ssm, state_conv, mem_prompt, g_mix, w_in, conv_w, conv_b, dt_bias, a_log, d_skip, g_ssd, lam_q1, lam_k1, lam_q2, lam_k2, g_subln, w_out, g_mem_q, g_mem_kv, w_mq, w_mk, w_mv, w_mo, g_ffn, w_pq, peer_keys, peer_u, peer_v, g_final):
    depth = w_in.shape[0]
    assert depth == 1, "the final norm is fused into the last layer's PEER kernel; one layer supported"
    bp, seq, d = x_prompt.shape
    _, _, nh, hd, ns = state_ssm.shape
    cdim = state_conv.shape[-1]
    width = nh * hd
    ng = (cdim - width) // (2 * ns)
    ah = cache_attn_k.shape[3]
    qk = lam_q1.shape[-1]
    dims = dict(ssd_heads=nh, ssd_head_dim=hd, ssd_state=ns, ssd_groups=ng, diff_heads=ah, qk_dim=qk,
                mem_heads=cache_mem_k.shape[3])
    sizes = (width, cdim, nh, ah * 2 * qk, ah * 2 * qk, ah * 2 * qk)
    splits = [sum(sizes[:i + 1]) for i in range(len(sizes) - 1)]
    n_mem = mem_prompt.shape[1]

    l = 0
    lam_init = 0.8 - 0.6 * math.exp(-0.3 * l)
    wz, wx, wdt, wq, wk, wv = jnp.split(w_in[l].astype(BF16), splits, axis=1)
    wdt = jnp.pad(wdt, ((0, 0), (0, LANES - nh)))
    pad_h = lambda a: jnp.pad(a.reshape(1, nh), ((0, 0), (0, LANES - nh)))
    expand = (jnp.arange(LANES)[:, None] == (jnp.arange(width) // hd)[None, :]).astype(F32)
    p = dict(
        g_mix=g_mix[l].reshape(1, d), w_in_parts=(wz, wx, wdt, wq, wk, wv),
        conv_w=conv_w[l], conv_b=conv_b[l].reshape(1, cdim), dt_bias_pad=pad_h(dt_bias[l]), a_log_pad=pad_h(a_log[l]),
        d_skip_x=jnp.repeat(d_skip[l], hd).reshape(1, width), g_ssd=g_ssd[l].reshape(1, width), expand=expand,
        lam_q1=lam_q1[l].reshape(1, qk), lam_k1=lam_k1[l].reshape(1, qk), lam_q2=lam_q2[l].reshape(1, qk),
        lam_k2=lam_k2[l].reshape(1, qk), g_subln=g_subln[l],
        w_out_a=w_out[l, :width].astype(BF16), w_out_b=w_out[l, width:].astype(BF16),
        g_mem_q=g_mem_q[l].reshape(1, d), w_mq=w_mq[l].astype(BF16), w_mo=w_mo[l].astype(BF16),
        g_ffn=g_ffn[l].reshape(1, d), w_pq_t=w_pq[l].T.astype(BF16), peer_keys=peer_keys[l].astype(BF16),
        peer_tab=jnp.concatenate([peer_u[l], peer_v[l]], axis=1), g_final=g_final.reshape(1, d),
    )

    mk_p, mv_p = _norm_mm2(mem_prompt.reshape(bp * n_mem, d), g_mem_kv[l].reshape(1, d), w_mk[l].astype(BF16),
                           w_mv[l].astype(BF16), tm=n_mem)
    mem_shape = (bp, n_mem) + cache_mem_k.shape[3:]
    mk_p, mv_p = mk_p.reshape(mem_shape), mv_p.reshape(mem_shape)

    tiles_p = dict(tm=min(512, seq), ssd_tb=min(512, seq), attn_tq=min(512, seq), mid_tm=min(512, seq),
                   peer_tm=256, gather_tt=8)
    h0_p = jnp.zeros((bp, nh, hd, ns), F32)
    conv0_p = jnp.zeros((bp, CONV_WIDTH - 1, cdim), F32)
    yp, kp, vp, hp, cp = _layer(x_prompt, 0, None, None, h0_p, conv0_p, mk_p, mv_p, p, lam_init, dims, tiles_p)

    bs, dseq, _ = x_sample.shape
    past = cache_attn_k.shape[2]
    ns_rows = bs * dseq
    tiles_s = dict(tm=min(512, ns_rows), ssd_tb=dseq, mid_tm=dseq, peer_tm=min(256, ns_rows), gather_tt=8)
    ys, ksn, vsn, hsn, csn = _layer(x_sample, past, cache_attn_k[l], cache_attn_v[l], state_ssm[l], state_conv[l],
                                    cache_mem_k[l], cache_mem_v[l], p, lam_init, dims, tiles_s)

    st = lambda a: a[None]
    return (yp, ys, st(kp), st(vp), st(hp), st(cp), st(mk_p), st(mv_p), st(ksn), st(vsn), st(hsn), st(csn))
```

```python
import functools
import math

import jax
import jax.numpy as jnp
from jax import lax
from jax.experimental import pallas as pl
from jax.experimental.pallas import tpu as pltpu

F32 = jnp.float32
BF16 = jnp.bfloat16
I32 = jnp.int32
EPS = 1e-6
LANES = 128
CHUNK = 64
ROT_HALF = 8
ROPE_THETA = 500000.0
CONV_WIDTH = 4
PEER_TOPK = 16
DMA_THREADS = 2
VMEM_LIMIT = 56 * 1024 * 1024
HIGHEST = lax.Precision.HIGHEST
NT_DIMS = (((1,), (1,)), ((), ()))


def _cparams(*sem):
    return pltpu.CompilerParams(dimension_semantics=sem, vmem_limit_bytes=VMEM_LIMIT)


def _rms(x, g):
    return x * lax.rsqrt(jnp.mean(x * x, axis=-1, keepdims=True) + EPS) * g


def _mm(a, b):
    return jnp.dot(a, b, preferred_element_type=F32)


def _mm_nt(a, b):
    return lax.dot_general(a, b, NT_DIMS, preferred_element_type=F32)


def _full(shape):
    return pl.BlockSpec(shape, lambda *_: (0,) * len(shape))


def _in_proj_kernel(x_ref, g_ref, wz_ref, wx_ref, wdt_ref, wq_ref, wk_ref, wv_ref, c_ref, s1_ref, s2_ref,
                    z_ref, xbc_ref, dt_ref, q_ref, k_ref, v_ref, kb_ref, vb_ref, *, q_scale):
    h = _rms(x_ref[...], g_ref[...]).astype(BF16)
    z_ref[...] = _mm(h, wz_ref[...])
    xbc_ref[...] = _mm(h, wx_ref[...])
    dt_ref[...] = _mm(h, wdt_ref[...])
    c, s1, s2 = c_ref[...], s1_ref[...], s2_ref[...]

    def rope(t):
        parts = []
        for gi in range(t.shape[1] // LANES):
            tg = t[:, gi * LANES:(gi + 1) * LANES]
            parts.append(tg * c + pltpu.roll(tg, ROT_HALF, 1) * s1 + pltpu.roll(tg, LANES - ROT_HALF, 1) * s2)
        return jnp.concatenate(parts, axis=1)

    q_ref[...] = (rope(_mm(h, wq_ref[...])) * q_scale).astype(BF16)
    k = rope(_mm(h, wk_ref[...]))
    k_ref[...] = k
    kb_ref[...] = k.astype(BF16)
    v = _mm(h, wv_ref[...])
    v_ref[...] = v
    vb_ref[...] = v.astype(BF16)


def _rope_tables(length, offset, qk_dim):
    inv = 1.0 / (ROPE_THETA ** (jnp.arange(ROT_HALF, dtype=F32) / ROT_HALF))
    pos = jnp.arange(length, dtype=F32) + offset
    ang = pos[:, None] * inv[None, :]
    cos, sin = jnp.cos(ang), jnp.sin(ang)
    rest = qk_dim - 2 * ROT_HALF
    one, zero = jnp.ones((length, rest), F32), jnp.zeros((length, rest), F32)
    z8 = jnp.zeros((length, ROT_HALF), F32)
    rep = LANES // qk_dim
    c = jnp.tile(jnp.concatenate([cos, cos, one], axis=1), (1, rep))
    s1 = jnp.tile(jnp.concatenate([z8, sin, zero], axis=1), (1, rep))
    s2 = jnp.tile(jnp.concatenate([-sin, z8, zero], axis=1), (1, rep))
    return c, s1, s2


def _in_proj(x, g, ws, tables, tm, rows_per_table_cycle, q_scale):
    n, d = x.shape
    wz, wx, wdt, wq, wk, wv = ws
    nt = rows_per_table_cycle // tm
    row = lambda w: pl.BlockSpec((tm, w), lambda i: (i, 0))
    tab = pl.BlockSpec((tm, LANES), lambda i: (i % nt, 0))
    widths = [wz.shape[1], wx.shape[1], wdt.shape[1], wq.shape[1], wk.shape[1], wv.shape[1]]
    out_shape = [jax.ShapeDtypeStruct((n, w), F32) for w in widths[:3]]
    out_shape += [jax.ShapeDtypeStruct((n, widths[3]), BF16), jax.ShapeDtypeStruct((n, widths[4]), F32),
                  jax.ShapeDtypeStruct((n, widths[5]), F32), jax.ShapeDtypeStruct((n, widths[4]), BF16),
                  jax.ShapeDtypeStruct((n, widths[5]), BF16)]
    out_specs = [row(w) for w in widths[:3]] + [row(widths[3]), row(widths[4]), row(widths[5]), row(widths[4]),
                                                 row(widths[5])]
    return pl.pallas_call(
        functools.partial(_in_proj_kernel, q_scale=q_scale),
        grid=(n // tm,),
        in_specs=[row(d), _full((1, d))] + [_full(w.shape) for w in ws] + [tab, tab, tab],
        out_specs=out_specs, out_shape=out_shape,
        compiler_params=_cparams("parallel"),
    )(x, g, *ws, *tables)


def _norm_mm2_kernel(x_ref, g_ref, wa_ref, wb_ref, a_ref, b_ref):
    h = _rms(x_ref[...], g_ref[...]).astype(BF16)
    a_ref[...] = _mm(h, wa_ref[...])
    b_ref[...] = _mm(h, wb_ref[...])


def _norm_mm2(x, g, wa, wb, tm):
    n, d = x.shape
    row = lambda w: pl.BlockSpec((tm, w), lambda i: (i, 0))
    return pl.pallas_call(
        _norm_mm2_kernel, grid=(n // tm,),
        in_specs=[row(d), _full((1, d)), _full(wa.shape), _full(wb.shape)],
        out_specs=[row(wa.shape[1]), row(wb.shape[1])],
        out_shape=[jax.ShapeDtypeStruct((n, wa.shape[1]), F32), jax.ShapeDtypeStruct((n, wb.shape[1]), F32)],
        compiler_params=_cparams("parallel"),
    )(x, g, wa, wb)


def _pad_rows_to_lanes(x):
    q = x.shape[0]
    if q == LANES:
        return x
    return jnp.concatenate([x, jnp.zeros((LANES - q, x.shape[1]), x.dtype)], axis=0)


def _ssd_kernel(xbc_ref, dt_ref, z_ref, h0_ref, cprev_ref, cw_ref, cb_ref, dtb_ref, alog_ref, dskip_ref, gssd_ref,
                expand_ref, y_ref, hl_ref, cn_ref, xp_ref, xc_ref, s_ref, *, q, tb, n_heads, head_dim, n_state,
                n_groups):
    t = pl.program_id(1)
    width = n_heads * head_dim
    gw = width // n_groups
    hpg = n_heads // n_groups
    halo = 8

    @pl.when(t == 0)
    def _():
        s_ref[...] = h0_ref[0]
        xp_ref[0:halo, :] = cprev_ref[0]

    xp_ref[halo:halo + tb, :] = xbc_ref[...]
    cw = cw_ref[...]
    conv = cb_ref[...]
    for j in range(CONV_WIDTH):
        off = halo - (CONV_WIDTH - 1) + j
        conv = conv + cw[j:j + 1, :] * xp_ref[off:off + tb, :]
    xc_ref[...] = conv * jax.nn.sigmoid(conv)
    cn_ref[0] = xp_ref[halo + tb - (CONV_WIDTH - 1):halo + tb, :]
    xp_ref[0:halo, :] = xp_ref[tb:tb + halo, :]

    a_neg = -jnp.exp(alog_ref[...])
    ri = lax.broadcasted_iota(I32, (q, q), 0)
    ci = lax.broadcasted_iota(I32, (q, q), 1)
    tril = ci <= ri
    tril_f = tril.astype(F32)
    lane_g = lax.broadcasted_iota(I32, (1, gw), 1)

    def chunk(c, carry):
        r0 = pl.multiple_of(c * q, q)
        xc = xc_ref[pl.ds(r0, q), :]
        xs = xc[:, :width]
        bm = xc[:, width:width + n_groups * n_state]
        cm = xc[:, width + n_groups * n_state:]
        dtr = dt_ref[pl.ds(r0, q), :] + dtb_ref[...]
        dt = jnp.maximum(dtr, 0.0) + jnp.log1p(jnp.exp(-jnp.abs(dtr)))
        da = dt * a_neg
        a_cum = jnp.dot(tril_f, da, precision=HIGHEST, preferred_element_type=F32)
        a_last = a_cum[q - 1:q, :]
        stk = jnp.concatenate([dt, jnp.exp(a_last - a_cum), jnp.exp(a_cum)], axis=0)
        ex = jnp.dot(stk, expand_ref[...], precision=HIGHEST, preferred_element_type=F32)
        dt_x, te_x, ec_x = ex[:q], ex[q:2 * q], ex[2 * q:]
        cd_x = ec_x[q - 1:q, :]
        xdt = xs * dt_x
        wgt = xdt * te_x
        a_cum_t = _pad_rows_to_lanes(a_cum).T
        ys = []
        for g in range(n_groups):
            bg = bm[:, g * n_state:(g + 1) * n_state]
            cg = cm[:, g * n_state:(g + 1) * n_state].astype(BF16)
            bg_t = _pad_rows_to_lanes(bg).T[:, :q].astype(BF16)
            cb = _mm_nt(cg, bg.astype(BF16))
            s_in = s_ref[g]
            gs = slice(g * gw, (g + 1) * gw)
            y_g = _mm(cg, s_in.astype(BF16)) * ec_x[:, gs] + dskip_ref[:, gs] * xs[:, gs]
            xdt_g = xdt[:, gs]
            for hl in range(hpg):
                h = g * hpg + hl
                seg = a_cum[:, h:h + 1] - a_cum_t[h:h + 1, :q]
                m_h = (cb * jnp.exp(jnp.where(tril, seg, -jnp.inf))).astype(BF16)
                in_head = (lane_g >= hl * head_dim) & (lane_g < (hl + 1) * head_dim)
                y_g = y_g + _mm(m_h, jnp.where(in_head, xdt_g, 0.0).astype(BF16))
            s_ref[g] = s_in * cd_x[:, gs] + _mm(bg_t, wgt[:, gs].astype(BF16))
            ys.append(y_g)
        y = jnp.concatenate(ys, axis=1)
        zc = z_ref[pl.ds(r0, q), :]
        y_ref[pl.ds(r0, q), :] = _rms(y * (zc * jax.nn.sigmoid(zc)), gssd_ref[...])
        return carry

    lax.fori_loop(0, tb // q, chunk, 0)

    @pl.when(t == pl.num_programs(1) - 1)
    def _():
        hl_ref[0] = s_ref[...]


def _ssd(xbc, dt, z, h0_t, cprev8, cw, cb, dtb, alog, dskip_x, gssd, expand, *, bsz, length, q, tb, n_heads,
         head_dim, n_state, n_groups):
    nt = length // tb
    width = n_heads * head_dim
    cdim = xbc.shape[1]
    row = lambda w: pl.BlockSpec((tb, w), lambda b, t: (b * nt + t, 0))
    gw = width // n_groups
    kern = functools.partial(_ssd_kernel, q=q, tb=tb, n_heads=n_heads, head_dim=head_dim, n_state=n_state,
                             n_groups=n_groups)
    return pl.pallas_call(
        kern, grid=(bsz, nt),
        in_specs=[row(cdim), row(LANES), row(width),
                  pl.BlockSpec((1, n_groups, n_state, gw), lambda b, t: (b, 0, 0, 0)),
                  pl.BlockSpec((1, 8, cdim), lambda b, t: (b, 0, 0)),
                  _full(cw.shape), _full(cb.shape), _full(dtb.shape), _full(alog.shape), _full(dskip_x.shape),
                  _full(gssd.shape), _full(expand.shape)],
        out_specs=[row(width),
                   pl.BlockSpec((1, n_groups, n_state, gw), lambda b, t: (b, 0, 0, 0)),
                   pl.BlockSpec((1, CONV_WIDTH - 1, cdim), lambda b, t: (b, 0, 0))],
        out_shape=[jax.ShapeDtypeStruct((bsz * length, width), F32),
                   jax.ShapeDtypeStruct((bsz, n_groups, n_state, gw), F32),
                   jax.ShapeDtypeStruct((bsz, CONV_WIDTH - 1, cdim), F32)],
        scratch_shapes=[pltpu.VMEM((tb + 8, cdim), F32), pltpu.VMEM((tb, cdim), F32),
                        pltpu.VMEM((n_groups, n_state, gw), F32)],
        compiler_params=_cparams("parallel", "arbitrary"),
    )(xbc, dt, z, h0_t, cprev8, cw, cb, dtb, alog, dskip_x, gssd, expand)


def _lambda(lq1_ref, lk1_ref, lq2_ref, lk2_ref, lam_init):
    e1 = jnp.exp(jnp.sum(lq1_ref[...] * lk1_ref[...], axis=1, keepdims=True))
    e2 = jnp.exp(jnp.sum(lq2_ref[...] * lk2_ref[...], axis=1, keepdims=True))
    return e1 - e2 + lam_init


def _split_components(qh, qk_dim):
    lane = lax.broadcasted_iota(I32, qh.shape, 1)
    zero = jnp.zeros_like(qh)
    return jnp.where(lane < qk_dim, qh, zero), jnp.where(lane >= qk_dim, qh, zero)


def _attn_prompt_kernel(q_ref, k_ref, vt_ref, lq1_ref, lk1_ref, lq2_ref, lk2_ref, gs_ref, o_ref,
                        acc1, acc2, m1, l1, m2, l2, s1_ref, s2_ref, *, tq, qk_dim, lam_init):
    qi = pl.program_id(2)
    qq1, qq2 = _split_components(q_ref[...], qk_dim)
    for acc, m, l in ((acc1, m1, l1), (acc2, m2, l2)):
        acc[...] = jnp.zeros_like(acc)
        m[...] = jnp.full_like(m, -jnp.inf)
        l[...] = jnp.zeros_like(l)

    def scores(j):
        kk = k_ref[pl.ds(pl.multiple_of(j * tq, tq), tq), :]
        return _mm_nt(kk, qq1), _mm_nt(kk, qq2)

    def consume(j, s_pair):
        vt = vt_ref[j]
        for s, acc, m, l in ((s_pair[0], acc1, m1, l1), (s_pair[1], acc2, m2, l2)):
            m_old = m[...]
            m_new = jnp.maximum(m_old, jnp.max(s, axis=0, keepdims=True))
            alpha = jnp.exp(m_old - m_new)
            p = jnp.exp(s - m_new)
            l[...] = alpha * l[...] + jnp.sum(p, axis=0, keepdims=True)
            acc[...] = alpha * acc[...] + _mm(vt, p.astype(BF16))
            m[...] = m_new

    s1_ref[...], s2_ref[...] = scores(0)

    def body(j, carry):
        cur = (s1_ref[...], s2_ref[...])
        nxt = scores(j + 1)
        consume(j, cur)
        s1_ref[...], s2_ref[...] = nxt
        return carry

    lax.fori_loop(0, qi, body, 0)
    kr = lax.broadcasted_iota(I32, (tq, tq), 0) // CHUNK
    qc = lax.broadcasted_iota(I32, (tq, tq), 1) // CHUNK
    visible = kr <= qc
    consume(qi, (jnp.where(visible, s1_ref[...], -jnp.inf), jnp.where(visible, s2_ref[...], -jnp.inf)))

    lam = _lambda(lq1_ref, lk1_ref, lq2_ref, lk2_ref, lam_init)
    o = acc1[...] * (1.0 / l1[...]) - lam * (acc2[...] * (1.0 / l2[...]))
    r = o * lax.rsqrt(jnp.mean(o * o, axis=0, keepdims=True) + EPS) * gs_ref[...]
    o_ref[...] = (r * (1.0 - lam_init)).T


def _attn_prompt(qb, kb, vt, lams, gs_col, *, tq, n_heads, qk_dim, lam_init):
    bsz, s, _ = qb.shape
    vd = vt.shape[3]
    nblk = s // tq
    kern = functools.partial(_attn_prompt_kernel, tq=tq, qk_dim=qk_dim, lam_init=lam_init)
    return pl.pallas_call(
        kern, grid=(bsz, n_heads, nblk),
        in_specs=[pl.BlockSpec((None, tq, 2 * qk_dim), lambda b, h, i: (b, i, h)),
                  pl.BlockSpec((None, s, 2 * qk_dim), lambda b, h, i: (b, 0, h)),
                  pl.BlockSpec((None, None, nblk, vd, tq), lambda b, h, i: (b, h, 0, 0, 0))]
                 + [_full(a.shape) for a in lams] + [_full(gs_col.shape)],
        out_specs=pl.BlockSpec((None, tq, vd), lambda b, h, i: (b, i, h)),
        out_shape=jax.ShapeDtypeStruct((bsz, s, n_heads * vd), F32),
        scratch_shapes=[pltpu.VMEM((vd, tq), F32), pltpu.VMEM((vd, tq), F32)] + [pltpu.VMEM((1, tq), F32)] * 4
                       + [pltpu.VMEM((tq, tq), F32)] * 2,
        compiler_params=_cparams("parallel", "parallel", "arbitrary"),
    )(qb, kb, vt, *lams, gs_col)


def _attn_sample_kernel(q_ref, kn_ref, vn_ref, kc_ref, vc_ref, lq1_ref, lk1_ref, lq2_ref, lk2_ref, gs_ref, o_ref,
                        *, n_heads, qk_dim, lam_init):
    lam = _lambda(lq1_ref, lk1_ref, lq2_ref, lk2_ref, lam_init)
    vd = 2 * qk_dim
    for h in range(n_heads):
        sl = slice(h * vd, (h + 1) * vd)
        qq = _split_components(q_ref[:, sl], qk_dim)
        kp = kc_ref[:, sl].astype(BF16)
        vp = vc_ref[:, sl].astype(BF16)
        kn = kn_ref[:, sl]
        vn = vn_ref[:, sl]
        outs = []
        for qc in qq:
            sp = _mm_nt(qc, kp)
            sn = _mm_nt(qc, kn)
            m = jnp.maximum(jnp.max(sp, axis=1, keepdims=True), jnp.max(sn, axis=1, keepdims=True))
            pp = jnp.exp(sp - m)
            pn = jnp.exp(sn - m)
            l = jnp.sum(pp, axis=1, keepdims=True) + jnp.sum(pn, axis=1, keepdims=True)
            outs.append((_mm(pp.astype(BF16), vp) + _mm(pn.astype(BF16), vn)) * (1.0 / l))
        o = outs[0] - lam * outs[1]
        o_ref[:, sl] = _rms(o, gs_ref[...]) * (1.0 - lam_init)


def _attn_sample(qb, kb, vb, kc, vc, lams, gs_row, *, n_heads, qk_dim, lam_init):
    bsz, length, w = qb.shape
    past = kc.shape[1]
    new = pl.BlockSpec((None, length, w), lambda b: (b, 0, 0))
    old = pl.BlockSpec((None, past, w), lambda b: (b, 0, 0))
    kern = functools.partial(_attn_sample_kernel, n_heads=n_heads, qk_dim=qk_dim, lam_init=lam_init)
    return pl.pallas_call(
        kern, grid=(bsz,),
        in_specs=[new, new, new, old, old] + [_full(a.shape) for a in lams] + [_full(gs_row.shape)],
        out_specs=new, out_shape=jax.ShapeDtypeStruct((bsz, length, w), F32),
        compiler_params=_cparams("parallel"),
    )(qb, kb, vb, kc, vc, *lams, gs_row)


def _mid_kernel(x_ref, ys_ref, oa_ref, woa_ref, wob_ref, gq_ref, wmq_ref, mk_ref, mv_ref, wmo_ref, x2_ref,
                *, n_heads, scale):
    x1 = x_ref[...] + _mm(ys_ref[...].astype(BF16), woa_ref[...]) + _mm(oa_ref[...].astype(BF16), wob_ref[...])
    qm = _mm(_rms(x1, gq_ref[...]).astype(BF16), wmq_ref[...])
    hd = qm.shape[1] // n_heads
    oms = []
    for h in range(n_heads):
        sl = slice(h * hd, (h + 1) * hd)
        s = _mm_nt(qm[:, sl].astype(BF16), mk_ref[:, sl].astype(BF16)) * scale
        p = jnp.exp(s - jnp.max(s, axis=1, keepdims=True))
        p = p * (1.0 / jnp.sum(p, axis=1, keepdims=True))
        oms.append(_mm(p.astype(BF16), mv_ref[:, sl].astype(BF16)))
    om = jnp.concatenate(oms, axis=1)
    x2_ref[...] = x1 + _mm(om.astype(BF16), wmo_ref[...])


def _mid(x, ys, oa, woa, wob, gq, wmq, mk, mv, wmo, *, tm, rows_per_batch, n_heads):
    n, d = x.shape
    tiles_per_batch = rows_per_batch // tm
    n_mem = mk.shape[1]
    row = lambda w: pl.BlockSpec((tm, w), lambda i: (i, 0))
    mem = pl.BlockSpec((None, n_mem, d), lambda i: (i // tiles_per_batch, 0, 0))
    kern = functools.partial(_mid_kernel, n_heads=n_heads, scale=1.0 / math.sqrt(d // n_heads))
    return pl.pallas_call(
        kern, grid=(n // tm,),
        in_specs=[row(d), row(ys.shape[1]), row(oa.shape[1]), _full(woa.shape), _full(wob.shape), _full(gq.shape),
                  _full(wmq.shape), mem, mem, _full(wmo.shape)],
        out_specs=row(d), out_shape=jax.ShapeDtypeStruct((n, d), F32),
        compiler_params=_cparams("parallel"),
    )(x, ys, oa, woa, wob, gq, wmq, mk, mv, wmo)


def _topk_rows(s, k):
    n, t = s.shape
    rows = lax.broadcasted_iota(I32, (n, t), 0)
    slot = lax.broadcasted_iota(I32, (k, t), 0)
    vals = jnp.zeros((k, t), F32)
    idxs = jnp.zeros((k, t), I32)
    for i in range(k):
        m = jnp.max(s, axis=0, keepdims=True)
        idx = jnp.min(jnp.where(s == m, rows, n), axis=0, keepdims=True)
        s = jnp.where(rows == idx, -jnp.inf, s)
        vals = jnp.where(slot == i, m, vals)
        idxs = jnp.where(slot == i, idx, idxs)
    return vals, idxs


def _pair_candidates(k):
    rows = [0 * k + j for j in range(k)]
    rows += [1 * k + j for j in range(8)]
    for i in range(2, 8):
        rows += [(i * k + j) if (i + 1) * (j + 1) <= k else -1 for j in range(8)]
    rows += [i * k for i in range(8, k)]
    return rows


def _peer_score_kernel(x2_ref, gf_ref, wpqt_ref, keys_ref, fid_ref, eid_ref, gate_ref, qt_ref, e_s, g_s,
                       *, n_heads, n_keys, topk):
    hf = _rms(x2_ref[...], gf_ref[...]).astype(BF16)
    qt_ref[...] = _mm_nt(wpqt_ref[...], hf).astype(BF16)
    half = wpqt_ref.shape[0] // (2 * n_heads)
    fid = fid_ref[...]
    tm = fid.shape[1]
    slot = lax.broadcasted_iota(I32, (topk, tm), 0)

    def head(h, carry):
        tops = []
        for c in range(2):
            r0 = pl.multiple_of((2 * h + c) * half, half)
            tops.append(_topk_rows(_mm(keys_ref[h, c], qt_ref[pl.ds(r0, half), :]), topk))
        (s1, i1), (s2, i2) = tops
        blocks_s = [s1[0:1] + s2, s1[1:2] + s2[0:8]]
        blocks_e = [i1[0:1] * n_keys + i2, i1[1:2] * n_keys + i2[0:8]]
        for i in range(2, 8):
            blocks_s.append(s1[i:i + 1] + s2[0:8])
            blocks_e.append(i1[i:i + 1] * n_keys + i2[0:8])
        blocks_s.append(s1[8:topk] + s2[0:1])
        blocks_e.append(i1[8:topk] * n_keys + i2[0:1])
        cand = jnp.where(fid >= 0, jnp.concatenate(blocks_s, axis=0), -jnp.inf)
        eid = jnp.concatenate(blocks_e, axis=0)
        sv = jnp.zeros((topk, tm), F32)
        ev = jnp.zeros((topk, tm), I32)
        for i in range(topk):
            m = jnp.max(cand, axis=0, keepdims=True)
            f = jnp.min(jnp.where(cand == m, fid, topk * topk), axis=0, keepdims=True)
            sel = fid == f
            e = jnp.max(jnp.where(sel, eid, -1), axis=0, keepdims=True)
            cand = jnp.where(sel, -jnp.inf, cand)
            sv = jnp.where(slot == i, m, sv)
            ev = jnp.where(slot == i, e, ev)
        p = jnp.exp(sv - sv[0:1])
        r0 = pl.multiple_of(h * topk, topk)
        e_s[pl.ds(r0, topk), :] = ev
        g_s[pl.ds(r0, topk), :] = p * (1.0 / jnp.sum(p, axis=0, keepdims=True))
        return carry

    lax.fori_loop(0, n_heads, head, 0)
    eid_ref[...] = e_s[...].T
    gate_ref[...] = g_s[...].T


def _peer_score(x2, gf, wpqt, keys, *, tm):
    n, d = x2.shape
    n_heads, _, n_keys, _ = keys.shape
    picks = n_heads * PEER_TOPK
    fid = jnp.tile(jnp.asarray(_pair_candidates(PEER_TOPK), I32)[:, None], (1, tm))
    kern = functools.partial(_peer_score_kernel, n_heads=n_heads, n_keys=n_keys, topk=PEER_TOPK)
    row = lambda w: pl.BlockSpec((tm, w), lambda i: (i, 0))
    return pl.pallas_call(
        kern, grid=(n // tm,),
        in_specs=[row(d), _full(gf.shape), _full(wpqt.shape), _full(keys.shape), _full(fid.shape)],
        out_specs=[row(picks), row(picks)],
        out_shape=[jax.ShapeDtypeStruct((n, picks), I32), jax.ShapeDtypeStruct((n, picks), F32)],
        scratch_shapes=[pltpu.VMEM((wpqt.shape[0], tm), BF16), pltpu.VMEM((picks, tm), I32),
                        pltpu.VMEM((picks, tm), F32)],
        compiler_params=_cparams("parallel"),
    )(x2, gf, wpqt, keys, fid)


def _gelu_tanh(x):
    return 0.5 * x * (1.0 + jnp.tanh(math.sqrt(2.0 / math.pi) * (x + 0.044715 * (x * x * x))))


def _peer_gather_kernel(eid_ref, eid_next_ref, gate_ref, x2_ref, gf_ref, gfin_ref, tab_ref, y_ref, buf0, buf1,
                        sems, *, tt, picks):
    dc = x2_ref.shape[1]
    d = dc * LANES
    i = pl.program_id(0)
    bufs = (buf0, buf1)
    group = 2

    def issue(idx_ref, row, s, t):
        for k in range(picks):
            e = idx_ref[row, k]
            pltpu.make_async_copy(tab_ref.at[e], bufs[s].at[t, k // 8, :, k % 8, :], sems.at[s, t]).start(
                priority=k % DMA_THREADS)

    def wait(s, t):
        pltpu.make_async_copy(bufs[1 - s].at[t], bufs[s].at[t], sems.at[s, t]).wait()

    def compute(s, t, row):
        x2 = x2_ref[row]
        ms = jnp.sum(jnp.sum(x2 * x2, axis=1, keepdims=True), axis=0, keepdims=True) * (1.0 / d)
        hf = x2 * lax.rsqrt(ms + EPS) * gf_ref[...]
        rows = bufs[s][t]
        pu = rows[:, 0] * hf[0:1, :]
        for c in range(1, dc):
            pu = pu + rows[:, c] * hf[c:c + 1, :]
        a = jnp.sum(pu, axis=2, keepdims=True)
        act = jnp.broadcast_to(_gelu_tanh(a) * gate_ref[:, :, row:row + 1], pu.shape)
        x3 = []
        for c in range(dc):
            oc = jnp.sum(jnp.sum(rows[:, dc + c] * act, axis=0), axis=0, keepdims=True)
            x3.append(x2[c:c + 1, :] + oc)
        ms3 = x3[0] * x3[0]
        for c in range(1, dc):
            ms3 = ms3 + x3[c] * x3[c]
        scale = lax.rsqrt(jnp.sum(ms3, axis=1, keepdims=True) * (1.0 / d) + EPS)
        for c in range(dc):
            y_ref[row, c:c + 1, :] = x3[c] * scale * gfin_ref[c:c + 1, :]

    @pl.when(i == 0)
    def _():
        def token(t, carry):
            issue(eid_ref, t, 0, t)
            return carry

        lax.fori_loop(0, tt, token, 0)

    for s, idx_ref, idx_row0, out_row0 in ((0, eid_ref, tt, 0), (1, eid_next_ref, 0, tt)):
        for t0 in range(0, tt, group):
            for t in range(t0, t0 + group):
                wait(s, t)
            for t in range(t0, t0 + group):
                issue(idx_ref, idx_row0 + t, 1 - s, t)
            for t in range(t0, t0 + group):
                compute(s, t, out_row0 + t)

    @pl.when(i == pl.num_programs(0) - 1)
    def _():
        for t in range(tt):
            wait(0, t)


def _peer_gather(eid, gate, x2, gf, gfin, tab, *, tt):
    n, d = x2.shape
    picks = eid.shape[1]
    dc = d // LANES
    steps = n // (2 * tt)
    tab3 = tab.reshape(tab.shape[0], tab.shape[1] // LANES, LANES)
    gate4 = gate.reshape(steps, 2 * tt, picks // 8, 8).transpose(0, 2, 3, 1)
    kern = functools.partial(_peer_gather_kernel, tt=tt, picks=picks)
    bshape = (tt, picks // 8, 2 * dc, 8, LANES)
    y = pl.pallas_call(
        kern, grid=(steps,),
        in_specs=[pl.BlockSpec((2 * tt, picks), lambda i: (i, 0), memory_space=pltpu.SMEM),
                  pl.BlockSpec((tt, picks), lambda i: (jnp.minimum(2 * i + 2, 2 * steps - 1), 0),
                               memory_space=pltpu.SMEM),
                  pl.BlockSpec((None, picks // 8, 8, 2 * tt), lambda i: (i, 0, 0, 0)),
                  pl.BlockSpec((2 * tt, dc, LANES), lambda i: (i, 0, 0)), _full((dc, LANES)), _full((dc, LANES)),
                  pl.BlockSpec(memory_space=pl.ANY)],
        out_specs=pl.BlockSpec((2 * tt, dc, LANES), lambda i: (i, 0, 0)),
        out_shape=jax.ShapeDtypeStruct((n, dc, LANES), F32),
        scratch_shapes=[pltpu.VMEM(bshape, F32), pltpu.VMEM(bshape, F32), pltpu.SemaphoreType.DMA((2, tt))],
        compiler_params=_cparams("arbitrary"),
    )(eid, eid, gate4, x2.reshape(n, dc, LANES), gf.reshape(dc, LANES), gfin.reshape(dc, LANES), tab3)
    return y.reshape(n, d)


def _layer(x, pos_offset, k_past, v_past, h0, conv_prev, mem_k, mem_v, p, lam_init, dims, tiles):
    bsz, length, d = x.shape
    n = bsz * length
    nh, hd, ns, ng = dims["ssd_heads"], dims["ssd_head_dim"], dims["ssd_state"], dims["ssd_groups"]
    ah, qk = dims["diff_heads"], dims["qk_dim"]
    vd = 2 * qk
    x2d = x.reshape(n, d)

    tables = _rope_tables(length, pos_offset, qk)
    tm = tiles["tm"]
    if tm > length:
        tables = tuple(jnp.tile(t, (tm // length, 1)) for t in tables)
        cycle = tm
    else:
        cycle = length
    z, xbc, dt, qb, k, v, kb, vb = _in_proj(x2d, p["g_mix"], p["w_in_parts"], tables, tm, cycle, 1.0 / math.sqrt(qk))

    gw = nh * hd // ng
    h0_t = h0.reshape(bsz, ng, nh // ng, hd, ns).transpose(0, 1, 4, 2, 3).reshape(bsz, ng, ns, gw)
    cprev8 = jnp.pad(conv_prev, ((0, 0), (8 - (CONV_WIDTH - 1), 0), (0, 0)))
    q_len = min(CHUNK, length)
    y_ssd, h_last_t, conv_new = _ssd(xbc, dt, z, h0_t, cprev8, p["conv_w"], p["conv_b"], p["dt_bias_pad"],
                                     p["a_log_pad"], p["d_skip_x"], p["g_ssd"], p["expand"], bsz=bsz, length=length,
                                     q=q_len, tb=tiles["ssd_tb"], n_heads=nh, head_dim=hd, n_state=ns, n_groups=ng)
    h_last = h_last_t.reshape(bsz, ng, ns, nh // ng, hd).transpose(0, 1, 3, 4, 2).reshape(bsz, nh, hd, ns)

    lams = (p["lam_q1"], p["lam_k1"], p["lam_q2"], p["lam_k2"])
    if k_past is None:
        tq = tiles["attn_tq"]
        vt = vb.reshape(bsz, length // tq, tq, ah, vd).transpose(0, 3, 1, 4, 2)
        o = _attn_prompt(qb.reshape(bsz, length, ah * vd), kb.reshape(bsz, length, ah * vd), vt, lams,
                         p["g_subln"].reshape(vd, 1), tq=tq, n_heads=ah, qk_dim=qk, lam_init=lam_init)
    else:
        past = k_past.shape[1]
        o = _attn_sample(qb.reshape(bsz, length, ah * vd), kb.reshape(bsz, length, ah * vd),
                         vb.reshape(bsz, length, ah * vd), k_past.reshape(bsz, past, ah * vd),
                         v_past.reshape(bsz, past, ah * vd), lams, p["g_subln"].reshape(1, vd), n_heads=ah,
                         qk_dim=qk, lam_init=lam_init)

    n_mem = mem_k.shape[1]
    x2 = _mid(x2d, y_ssd, o.reshape(n, ah * vd), p["w_out_a"], p["w_out_b"], p["g_mem_q"], p["w_mq"],
              mem_k.reshape(bsz, n_mem, d), mem_v.reshape(bsz, n_mem, d), p["w_mo"], tm=tiles["mid_tm"],
              rows_per_batch=length, n_heads=dims["mem_heads"])

    eid, gate = _peer_score(x2, p["g_ffn"], p["w_pq_t"], p["peer_keys"], tm=tiles["peer_tm"])
    tt = tiles["gather_tt"]
    y = _peer_gather(eid, gate, x2, p["g_ffn"], p["g_final"], p["peer_tab"], tt=tt)
    return (y.reshape(bsz, length, d), k.reshape(bsz, length, ah, vd), v.reshape(bsz, length, ah, vd), h_last,
            conv_new)


def kernel(x_prompt, x_sample, cache_attn_k, cache_attn_v, cache_mem_k, cache_mem_v, state_ssm, state_conv, mem_prompt, g_mix, w_in, conv_w, conv_b, dt_bias, a_log, d_skip, g_ssd, lam_q1, lam_k1, lam_q2, lam_k2, g_subln, w_out, g_mem_q, g_mem_kv, w_mq, w_mk, w_mv, w_mo, g_ffn, w_pq, peer_keys, peer_u, peer_v, g_final):
    depth = w_in.shape[0]
    assert depth == 1, "the final norm is fused into the last layer's PEER kernel; one layer supported"
    bp, seq, d = x_prompt.shape
    _, _, nh, hd, ns = state_ssm.shape
    cdim = state_conv.shape[-1]
    width = nh * hd
    ng = (cdim - width) // (2 * ns)
    ah = cache_attn_k.shape[3]
    qk = lam_q1.shape[-1]
    dims = dict(ssd_heads=nh, ssd_head_dim=hd, ssd_state=ns, ssd_groups=ng, diff_heads=ah, qk_dim=qk,
                mem_heads=cache_mem_k.shape[3])
    sizes = (width, cdim, nh, ah * 2 * qk, ah * 2 * qk, ah * 2 * qk)
    splits = [sum(sizes[:i + 1]) for i in range(len(sizes) - 1)]
    n_mem = mem_prompt.shape[1]

    l = 0
    lam_init = 0.8 - 0.6 * math.exp(-0.3 * l)
    wz, wx, wdt, wq, wk, wv = jnp.split(w_in[l].astype(BF16), splits, axis=1)
    wdt = jnp.pad(wdt, ((0, 0), (0, LANES - nh)))
    pad_h = lambda a: jnp.pad(a.reshape(1, nh), ((0, 0), (0, LANES - nh)))
    expand = (jnp.arange(LANES)[:, None] == (jnp.arange(width) // hd)[None, :]).astype(F32)
    p = dict(
        g_mix=g_mix[l].reshape(1, d), w_in_parts=(wz, wx, wdt, wq, wk, wv),
        conv_w=conv_w[l], conv_b=conv_b[l].reshape(1, cdim), dt_bias_pad=pad_h(dt_bias[l]), a_log_pad=pad_h(a_log[l]),
        d_skip_x=jnp.repeat(d_skip[l], hd).reshape(1, width), g_ssd=g_ssd[l].reshape(1, width), expand=expand,
        lam_q1=lam_q1[l].reshape(1, qk), lam_k1=lam_k1[l].reshape(1, qk), lam_q2=lam_q2[l].reshape(1, qk),
        lam_k2=lam_k2[l].reshape(1, qk), g_subln=g_subln[l],
        w_out_a=w_out[l, :width].astype(BF16), w_out_b=w_out[l, width:].astype(BF16),
        g_mem_q=g_mem_q[l].reshape(1, d), w_mq=w_mq[l].astype(BF16), w_mo=w_mo[l].astype(BF16),
        g_ffn=g_ffn[l].reshape(1, d), w_pq_t=w_pq[l].T.astype(BF16), peer_keys=peer_keys[l].astype(BF16),
        peer_tab=jnp.concatenate([peer_u[l], peer_v[l]], axis=1), g_final=g_final.reshape(1, d),
    )

    mk_p, mv_p = _norm_mm2(mem_prompt.reshape(bp * n_mem, d), g_mem_kv[l].reshape(1, d), w_mk[l].astype(BF16),
                           w_mv[l].astype(BF16), tm=n_mem)
    mem_shape = (bp, n_mem) + cache_mem_k.shape[3:]
    mk_p, mv_p = mk_p.reshape(mem_shape), mv_p.reshape(mem_shape)

    tiles_p = dict(tm=min(512, seq), ssd_tb=min(512, seq), attn_tq=min(512, seq), mid_tm=min(512, seq),
                   peer_tm=512, gather_tt=8)
    h0_p = jnp.zeros((bp, nh, hd, ns), F32)
    conv0_p = jnp.zeros((bp, CONV_WIDTH - 1, cdim), F32)
    yp, kp, vp, hp, cp = _layer(x_prompt, 0, None, None, h0_p, conv0_p, mk_p, mv_p, p, lam_init, dims, tiles_p)

    bs, dseq, _ = x_sample.shape
    past = cache_attn_k.shape[2]
    ns_rows = bs * dseq
    tiles_s = dict(tm=min(512, ns_rows), ssd_tb=dseq, mid_tm=dseq, peer_tm=min(256, ns_rows), gather_tt=8)
    ys, ksn, vsn, hsn, csn = _layer(x_sample, past, cache_attn_k[l], cache_attn_v[l], state_ssm[l], state_conv[l],
                                    cache_mem_k[l], cache_mem_v[l], p, lam_init, dims, tiles_s)

    st = lambda a: a[None]
    return (yp, ys, st(kp), st(vp), st(hp), st(cp), st(mk_p), st(mv_p), st(ksn), st(vsn), st(hsn), st(csn))
```

```python
import functools
import math

import jax
import jax.numpy as jnp
from jax import lax
from jax.experimental import pallas as pl
from jax.experimental.pallas import tpu as pltpu

F32 = jnp.float32
BF16 = jnp.bfloat16
I32 = jnp.int32
EPS = 1e-6
LANES = 128
CHUNK = 64
ROT_HALF = 8
ROPE_THETA = 500000.0
CONV_WIDTH = 4
PEER_TOPK = 16
DMA_THREADS = 2
VMEM_LIMIT = 56 * 1024 * 1024
HIGHEST = lax.Precision.HIGHEST
NT_DIMS = (((1,), (1,)), ((), ()))


def _cparams(*sem):
    return pltpu.CompilerParams(dimension_semantics=sem, vmem_limit_bytes=VMEM_LIMIT)


def _rms(x, g):
    return x * lax.rsqrt(jnp.mean(x * x, axis=-1, keepdims=True) + EPS) * g


def _mm(a, b):
    return jnp.dot(a, b, preferred_element_type=F32)


def _mm_nt(a, b):
    return lax.dot_general(a, b, NT_DIMS, preferred_element_type=F32)


def _full(shape):
    return pl.BlockSpec(shape, lambda *_: (0,) * len(shape))


def _in_proj_kernel(x_ref, g_ref, wz_ref, wx_ref, wdt_ref, wq_ref, wk_ref, wv_ref, c_ref, s1_ref, s2_ref,
                    z_ref, xbc_ref, dt_ref, q_ref, k_ref, v_ref, kb_ref, vb_ref, *, q_scale):
    h = _rms(x_ref[...], g_ref[...]).astype(BF16)
    z_ref[...] = _mm(h, wz_ref[...])
    xbc_ref[...] = _mm(h, wx_ref[...])
    dt_ref[...] = _mm(h, wdt_ref[...])
    c, s1, s2 = c_ref[...], s1_ref[...], s2_ref[...]

    def rope(t):
        parts = []
        for gi in range(t.shape[1] // LANES):
            tg = t[:, gi * LANES:(gi + 1) * LANES]
            parts.append(tg * c + pltpu.roll(tg, ROT_HALF, 1) * s1 + pltpu.roll(tg, LANES - ROT_HALF, 1) * s2)
        return jnp.concatenate(parts, axis=1)

    q_ref[...] = (rope(_mm(h, wq_ref[...])) * q_scale).astype(BF16)
    k = rope(_mm(h, wk_ref[...]))
    k_ref[...] = k
    kb_ref[...] = k.astype(BF16)
    v = _mm(h, wv_ref[...])
    v_ref[...] = v
    vb_ref[...] = v.astype(BF16)


def _rope_tables(length, offset, qk_dim):
    inv = 1.0 / (ROPE_THETA ** (jnp.arange(ROT_HALF, dtype=F32) / ROT_HALF))
    pos = jnp.arange(length, dtype=F32) + offset
    ang = pos[:, None] * inv[None, :]
    cos, sin = jnp.cos(ang), jnp.sin(ang)
    rest = qk_dim - 2 * ROT_HALF
    one, zero = jnp.ones((length, rest), F32), jnp.zeros((length, rest), F32)
    z8 = jnp.zeros((length, ROT_HALF), F32)
    rep = LANES // qk_dim
    c = jnp.tile(jnp.concatenate([cos, cos, one], axis=1), (1, rep))
    s1 = jnp.tile(jnp.concatenate([z8, sin, zero], axis=1), (1, rep))
    s2 = jnp.tile(jnp.concatenate([-sin, z8, zero], axis=1), (1, rep))
    return c, s1, s2


def _in_proj(x, g, ws, tables, tm, rows_per_table_cycle, q_scale):
    n, d = x.shape
    wz, wx, wdt, wq, wk, wv = ws
    nt = rows_per_table_cycle // tm
    row = lambda w: pl.BlockSpec((tm, w), lambda i: (i, 0))
    tab = pl.BlockSpec((tm, LANES), lambda i: (i % nt, 0))
    widths = [wz.shape[1], wx.shape[1], wdt.shape[1], wq.shape[1], wk.shape[1], wv.shape[1]]
    out_shape = [jax.ShapeDtypeStruct((n, w), F32) for w in widths[:3]]
    out_shape += [jax.ShapeDtypeStruct((n, widths[3]), BF16), jax.ShapeDtypeStruct((n, widths[4]), F32),
                  jax.ShapeDtypeStruct((n, widths[5]), F32), jax.ShapeDtypeStruct((n, widths[4]), BF16),
                  jax.ShapeDtypeStruct((n, widths[5]), BF16)]
    out_specs = [row(w) for w in widths[:3]] + [row(widths[3]), row(widths[4]), row(widths[5]), row(widths[4]),
                                                 row(widths[5])]
    return pl.pallas_call(
        functools.partial(_in_proj_kernel, q_scale=q_scale),
        grid=(n // tm,),
        in_specs=[row(d), _full((1, d))] + [_full(w.shape) for w in ws] + [tab, tab, tab],
        out_specs=out_specs, out_shape=out_shape,
        compiler_params=_cparams("parallel"),
    )(x, g, *ws, *tables)


def _norm_mm2_kernel(x_ref, g_ref, wa_ref, wb_ref, a_ref, b_ref):
    h = _rms(x_ref[...], g_ref[...]).astype(BF16)
    a_ref[...] = _mm(h, wa_ref[...])
    b_ref[...] = _mm(h, wb_ref[...])


def _norm_mm2(x, g, wa, wb, tm):
    n, d = x.shape
    row = lambda w: pl.BlockSpec((tm, w), lambda i: (i, 0))
    return pl.pallas_call(
        _norm_mm2_kernel, grid=(n // tm,),
        in_specs=[row(d), _full((1, d)), _full(wa.shape), _full(wb.shape)],
        out_specs=[row(wa.shape[1]), row(wb.shape[1])],
        out_shape=[jax.ShapeDtypeStruct((n, wa.shape[1]), F32), jax.ShapeDtypeStruct((n, wb.shape[1]), F32)],
        compiler_params=_cparams("parallel"),
    )(x, g, wa, wb)


def _pad_rows_to_lanes(x):
    q = x.shape[0]
    if q == LANES:
        return x
    return jnp.concatenate([x, jnp.zeros((LANES - q, x.shape[1]), x.dtype)], axis=0)


def _ssd_kernel(xbc_ref, dt_ref, z_ref, h0_ref, cprev_ref, cw_ref, cb_ref, dtb_ref, alog_ref, dskip_ref, gssd_ref,
                expand_ref, y_ref, hl_ref, cn_ref, xp_ref, xc_ref, s_ref, *, q, tb, n_heads, head_dim, n_state,
                n_groups):
    t = pl.program_id(1)
    width = n_heads * head_dim
    gw = width // n_groups
    hpg = n_heads // n_groups
    halo = 8

    @pl.when(t == 0)
    def _():
        s_ref[...] = h0_ref[0]
        xp_ref[0:halo, :] = cprev_ref[0]

    xp_ref[halo:halo + tb, :] = xbc_ref[...]
    cw = cw_ref[...]
    conv = cb_ref[...]
    for j in range(CONV_WIDTH):
        off = halo - (CONV_WIDTH - 1) + j
        conv = conv + cw[j:j + 1, :] * xp_ref[off:off + tb, :]
    xc_ref[...] = conv * jax.nn.sigmoid(conv)
    cn_ref[0] = xp_ref[halo + tb - (CONV_WIDTH - 1):halo + tb, :]
    xp_ref[0:halo, :] = xp_ref[tb:tb + halo, :]

    a_neg = -jnp.exp(alog_ref[...])
    ri = lax.broadcasted_iota(I32, (q, q), 0)
    ci = lax.broadcasted_iota(I32, (q, q), 1)
    tril = ci <= ri
    tril_f = tril.astype(F32)
    lane_g = lax.broadcasted_iota(I32, (1, gw), 1)

    def chunk(c, carry):
        r0 = pl.multiple_of(c * q, q)
        xc = xc_ref[pl.ds(r0, q), :]
        xs = xc[:, :width]
        bm = xc[:, width:width + n_groups * n_state]
        cm = xc[:, width + n_groups * n_state:]
        dtr = dt_ref[pl.ds(r0, q), :] + dtb_ref[...]
        dt = jnp.maximum(dtr, 0.0) + jnp.log1p(jnp.exp(-jnp.abs(dtr)))
        da = dt * a_neg
        a_cum = jnp.dot(tril_f, da, precision=HIGHEST, preferred_element_type=F32)
        a_last = a_cum[q - 1:q, :]
        stk = jnp.concatenate([dt, jnp.exp(a_last - a_cum), jnp.exp(a_cum)], axis=0)
        ex = jnp.dot(stk, expand_ref[...], precision=HIGHEST, preferred_element_type=F32)
        dt_x, te_x, ec_x = ex[:q], ex[q:2 * q], ex[2 * q:]
        cd_x = ec_x[q - 1:q, :]
        xdt = xs * dt_x
        wgt = xdt * te_x
        a_cum_t = _pad_rows_to_lanes(a_cum).T
        ys = []
        for g in range(n_groups):
            bg = bm[:, g * n_state:(g + 1) * n_state]
            cg = cm[:, g * n_state:(g + 1) * n_state].astype(BF16)
            bg_t = _pad_rows_to_lanes(bg).T[:, :q].astype(BF16)
            cb = _mm_nt(cg, bg.astype(BF16))
            s_in = s_ref[g]
            gs = slice(g * gw, (g + 1) * gw)
            y_g = _mm(cg, s_in.astype(BF16)) * ec_x[:, gs] + dskip_ref[:, gs] * xs[:, gs]
            xdt_g = xdt[:, gs]
            for hl in range(hpg):
                h = g * hpg + hl
                seg = a_cum[:, h:h + 1] - a_cum_t[h:h + 1, :q]
                m_h = (cb * jnp.exp(jnp.where(tril, seg, -jnp.inf))).astype(BF16)
                in_head = (lane_g >= hl * head_dim) & (lane_g < (hl + 1) * head_dim)
                y_g = y_g + _mm(m_h, jnp.where(in_head, xdt_g, 0.0).astype(BF16))
            s_ref[g] = s_in * cd_x[:, gs] + _mm(bg_t, wgt[:, gs].astype(BF16))
            ys.append(y_g)
        y = jnp.concatenate(ys, axis=1)
        zc = z_ref[pl.ds(r0, q), :]
        y_ref[pl.ds(r0, q), :] = _rms(y * (zc * jax.nn.sigmoid(zc)), gssd_ref[...])
        return carry

    lax.fori_loop(0, tb // q, chunk, 0)

    @pl.when(t == pl.num_programs(1) - 1)
    def _():
        hl_ref[0] = s_ref[...]


def _ssd(xbc, dt, z, h0_t, cprev8, cw, cb, dtb, alog, dskip_x, gssd, expand, *, bsz, length, q, tb, n_heads,
         head_dim, n_state, n_groups):
    nt = length // tb
    width = n_heads * head_dim
    cdim = xbc.shape[1]
    row = lambda w: pl.BlockSpec((tb, w), lambda b, t: (b * nt + t, 0))
    gw = width // n_groups
    kern = functools.partial(_ssd_kernel, q=q, tb=tb, n_heads=n_heads, head_dim=head_dim, n_state=n_state,
                             n_groups=n_groups)
    return pl.pallas_call(
        kern, grid=(bsz, nt),
        in_specs=[row(cdim), row(LANES), row(width),
                  pl.BlockSpec((1, n_groups, n_state, gw), lambda b, t: (b, 0, 0, 0)),
                  pl.BlockSpec((1, 8, cdim), lambda b, t: (b, 0, 0)),
                  _full(cw.shape), _full(cb.shape), _full(dtb.shape), _full(alog.shape), _full(dskip_x.shape),
                  _full(gssd.shape), _full(expand.shape)],
        out_specs=[row(width),
                   pl.BlockSpec((1, n_groups, n_state, gw), lambda b, t: (b, 0, 0, 0)),
                   pl.BlockSpec((1, CONV_WIDTH - 1, cdim), lambda b, t: (b, 0, 0))],
        out_shape=[jax.ShapeDtypeStruct((bsz * length, width), F32),
                   jax.ShapeDtypeStruct((bsz, n_groups, n_state, gw), F32),
                   jax.ShapeDtypeStruct((bsz, CONV_WIDTH - 1, cdim), F32)],
        scratch_shapes=[pltpu.VMEM((tb + 8, cdim), F32), pltpu.VMEM((tb, cdim), F32),
                        pltpu.VMEM((n_groups, n_state, gw), F32)],
        compiler_params=_cparams("parallel", "arbitrary"),
    )(xbc, dt, z, h0_t, cprev8, cw, cb, dtb, alog, dskip_x, gssd, expand)


def _lambda(lq1_ref, lk1_ref, lq2_ref, lk2_ref, lam_init):
    e1 = jnp.exp(jnp.sum(lq1_ref[...] * lk1_ref[...], axis=1, keepdims=True))
    e2 = jnp.exp(jnp.sum(lq2_ref[...] * lk2_ref[...], axis=1, keepdims=True))
    return e1 - e2 + lam_init


def _split_components(qh, qk_dim):
    lane = lax.broadcasted_iota(I32, qh.shape, 1)
    zero = jnp.zeros_like(qh)
    return jnp.where(lane < qk_dim, qh, zero), jnp.where(lane >= qk_dim, qh, zero)


def _attn_prompt_kernel(q_ref, k_ref, vt_ref, lq1_ref, lk1_ref, lq2_ref, lk2_ref, gs_ref, o_ref,
                        acc1, acc2, m1, l1, m2, l2, s1_ref, s2_ref, *, tq, qk_dim, lam_init):
    qi = pl.program_id(2)
    qq1, qq2 = _split_components(q_ref[...], qk_dim)
    for acc, m, l in ((acc1, m1, l1), (acc2, m2, l2)):
        acc[...] = jnp.zeros_like(acc)
        m[...] = jnp.full_like(m, -jnp.inf)
        l[...] = jnp.zeros_like(l)

    def scores(j):
        kk = k_ref[pl.ds(pl.multiple_of(j * tq, tq), tq), :]
        return _mm_nt(kk, qq1), _mm_nt(kk, qq2)

    def consume(j, s_pair):
        vt = vt_ref[j]
        for s, acc, m, l in ((s_pair[0], acc1, m1, l1), (s_pair[1], acc2, m2, l2)):
            m_old = m[...]
            m_new = jnp.maximum(m_old, jnp.max(s, axis=0, keepdims=True))
            alpha = jnp.exp(m_old - m_new)
            p = jnp.exp(s - m_new)
            l[...] = alpha * l[...] + jnp.sum(p, axis=0, keepdims=True)
            acc[...] = alpha * acc[...] + _mm(vt, p.astype(BF16))
            m[...] = m_new

    s1_ref[...], s2_ref[...] = scores(0)

    def body(j, carry):
        cur = (s1_ref[...], s2_ref[...])
        nxt = scores(j + 1)
        consume(j, cur)
        s1_ref[...], s2_ref[...] = nxt
        return carry

    lax.fori_loop(0, qi, body, 0)
    kr = lax.broadcasted_iota(I32, (tq, tq), 0) // CHUNK
    qc = lax.broadcasted_iota(I32, (tq, tq), 1) // CHUNK
    visible = kr <= qc
    consume(qi, (jnp.where(visible, s1_ref[...], -jnp.inf), jnp.where(visible, s2_ref[...], -jnp.inf)))

    lam = _lambda(lq1_ref, lk1_ref, lq2_ref, lk2_ref, lam_init)
    o = acc1[...] * (1.0 / l1[...]) - lam * (acc2[...] * (1.0 / l2[...]))
    r = o * lax.rsqrt(jnp.mean(o * o, axis=0, keepdims=True) + EPS) * gs_ref[...]
    o_ref[...] = (r * (1.0 - lam_init)).T


def _attn_prompt(qb, kb, vt, lams, gs_col, *, tq, n_heads, qk_dim, lam_init):
    bsz, s, _ = qb.shape
    vd = vt.shape[3]
    nblk = s // tq
    kern = functools.partial(_attn_prompt_kernel, tq=tq, qk_dim=qk_dim, lam_init=lam_init)
    return pl.pallas_call(
        kern, grid=(bsz, n_heads, nblk),
        in_specs=[pl.BlockSpec((None, tq, 2 * qk_dim), lambda b, h, i: (b, i, h)),
                  pl.BlockSpec((None, s, 2 * qk_dim), lambda b, h, i: (b, 0, h)),
                  pl.BlockSpec((None, None, nblk, vd, tq), lambda b, h, i: (b, h, 0, 0, 0))]
                 + [_full(a.shape) for a in lams] + [_full(gs_col.shape)],
        out_specs=pl.BlockSpec((None, tq, vd), lambda b, h, i: (b, i, h)),
        out_shape=jax.ShapeDtypeStruct((bsz, s, n_heads * vd), F32),
        scratch_shapes=[pltpu.VMEM((vd, tq), F32), pltpu.VMEM((vd, tq), F32)] + [pltpu.VMEM((1, tq), F32)] * 4
                       + [pltpu.VMEM((tq, tq), F32)] * 2,
        compiler_params=_cparams("parallel", "parallel", "arbitrary"),
    )(qb, kb, vt, *lams, gs_col)


def _attn_sample_kernel(q_ref, kn_ref, vn_ref, kc_ref, vc_ref, lq1_ref, lk1_ref, lq2_ref, lk2_ref, gs_ref, o_ref,
                        *, n_heads, qk_dim, lam_init):
    lam = _lambda(lq1_ref, lk1_ref, lq2_ref, lk2_ref, lam_init)
    vd = 2 * qk_dim
    for h in range(n_heads):
        sl = slice(h * vd, (h + 1) * vd)
        qq = _split_components(q_ref[:, sl], qk_dim)
        kp = kc_ref[:, sl].astype(BF16)
        vp = vc_ref[:, sl].astype(BF16)
        kn = kn_ref[:, sl]
        vn = vn_ref[:, sl]
        outs = []
        for qc in qq:
            sp = _mm_nt(qc, kp)
            sn = _mm_nt(qc, kn)
            m = jnp.maximum(jnp.max(sp, axis=1, keepdims=True), jnp.max(sn, axis=1, keepdims=True))
            pp = jnp.exp(sp - m)
            pn = jnp.exp(sn - m)
            l = jnp.sum(pp, axis=1, keepdims=True) + jnp.sum(pn, axis=1, keepdims=True)
            outs.append((_mm(pp.astype(BF16), vp) + _mm(pn.astype(BF16), vn)) * (1.0 / l))
        o = outs[0] - lam * outs[1]
        o_ref[:, sl] = _rms(o, gs_ref[...]) * (1.0 - lam_init)


def _attn_sample(qb, kb, vb, kc, vc, lams, gs_row, *, n_heads, qk_dim, lam_init):
    bsz, length, w = qb.shape
    past = kc.shape[1]
    new = pl.BlockSpec((None, length, w), lambda b: (b, 0, 0))
    old = pl.BlockSpec((None, past, w), lambda b: (b, 0, 0))
    kern = functools.partial(_attn_sample_kernel, n_heads=n_heads, qk_dim=qk_dim, lam_init=lam_init)
    return pl.pallas_call(
        kern, grid=(bsz,),
        in_specs=[new, new, new, old, old] + [_full(a.shape) for a in lams] + [_full(gs_row.shape)],
        out_specs=new, out_shape=jax.ShapeDtypeStruct((bsz, length, w), F32),
        compiler_params=_cparams("parallel"),
    )(qb, kb, vb, kc, vc, *lams, gs_row)


def _mid_kernel(x_ref, ys_ref, oa_ref, woa_ref, wob_ref, gq_ref, wmq_ref, mk_ref, mv_ref, wmo_ref, x2_ref,
                *, n_heads, scale):
    x1 = x_ref[...] + _mm(ys_ref[...].astype(BF16), woa_ref[...]) + _mm(oa_ref[...].astype(BF16), wob_ref[...])
    qm = _mm(_rms(x1, gq_ref[...]).astype(BF16), wmq_ref[...])
    hd = qm.shape[1] // n_heads
    oms = []
    for h in range(n_heads):
        sl = slice(h * hd, (h + 1) * hd)
        s = _mm_nt(qm[:, sl].astype(BF16), mk_ref[:, sl].astype(BF16)) * scale
        p = jnp.exp(s - jnp.max(s, axis=1, keepdims=True))
        p = p * (1.0 / jnp.sum(p, axis=1, keepdims=True))
        oms.append(_mm(p.astype(BF16), mv_ref[:, sl].astype(BF16)))
    om = jnp.concatenate(oms, axis=1)
    x2_ref[...] = x1 + _mm(om.astype(BF16), wmo_ref[...])


def _mid(x, ys, oa, woa, wob, gq, wmq, mk, mv, wmo, *, tm, rows_per_batch, n_heads):
    n, d = x.shape
    tiles_per_batch = rows_per_batch // tm
    n_mem = mk.shape[1]
    row = lambda w: pl.BlockSpec((tm, w), lambda i: (i, 0))
    mem = pl.BlockSpec((None, n_mem, d), lambda i: (i // tiles_per_batch, 0, 0))
    kern = functools.partial(_mid_kernel, n_heads=n_heads, scale=1.0 / math.sqrt(d // n_heads))
    return pl.pallas_call(
        kern, grid=(n // tm,),
        in_specs=[row(d), row(ys.shape[1]), row(oa.shape[1]), _full(woa.shape), _full(wob.shape), _full(gq.shape),
                  _full(wmq.shape), mem, mem, _full(wmo.shape)],
        out_specs=row(d), out_shape=jax.ShapeDtypeStruct((n, d), F32),
        compiler_params=_cparams("parallel"),
    )(x, ys, oa, woa, wob, gq, wmq, mk, mv, wmo)


def _topk_rows(s, k):
    n, t = s.shape
    rows = lax.broadcasted_iota(I32, (n, t), 0)
    slot = lax.broadcasted_iota(I32, (k, t), 0)
    vals = jnp.zeros((k, t), F32)
    idxs = jnp.zeros((k, t), I32)
    for i in range(k):
        m = jnp.max(s, axis=0, keepdims=True)
        idx = jnp.min(jnp.where(s == m, rows, n), axis=0, keepdims=True)
        s = jnp.where(rows == idx, -jnp.inf, s)
        vals = jnp.where(slot == i, m, vals)
        idxs = jnp.where(slot == i, idx, idxs)
    return vals, idxs


def _pair_candidates(k):
    rows = [0 * k + j for j in range(k)]
    rows += [1 * k + j for j in range(8)]
    for i in range(2, 8):
        rows += [(i * k + j) if (i + 1) * (j + 1) <= k else -1 for j in range(8)]
    rows += [i * k for i in range(8, k)]
    return rows


def _peer_score_kernel(x2_ref, gf_ref, wpqt_ref, keys_ref, fid_ref, eid_ref, gate_ref, qt_ref, e_s, g_s,
                       *, n_heads, n_keys, topk):
    hf = _rms(x2_ref[...], gf_ref[...]).astype(BF16)
    qt_ref[...] = _mm_nt(wpqt_ref[...], hf).astype(BF16)
    half = wpqt_ref.shape[0] // (2 * n_heads)
    fid = fid_ref[...]
    tm = fid.shape[1]
    slot = lax.broadcasted_iota(I32, (topk, tm), 0)

    def head(h, carry):
        tops = []
        for c in range(2):
            r0 = pl.multiple_of((2 * h + c) * half, half)
            tops.append(_topk_rows(_mm(keys_ref[h, c], qt_ref[pl.ds(r0, half), :]), topk))
        (s1, i1), (s2, i2) = tops
        blocks_s = [s1[0:1] + s2, s1[1:2] + s2[0:8]]
        blocks_e = [i1[0:1] * n_keys + i2, i1[1:2] * n_keys + i2[0:8]]
        for i in range(2, 8):
            blocks_s.append(s1[i:i + 1] + s2[0:8])
            blocks_e.append(i1[i:i + 1] * n_keys + i2[0:8])
        blocks_s.append(s1[8:topk] + s2[0:1])
        blocks_e.append(i1[8:topk] * n_keys + i2[0:1])
        cand = jnp.where(fid >= 0, jnp.concatenate(blocks_s, axis=0), -jnp.inf)
        eid = jnp.concatenate(blocks_e, axis=0)
        sv = jnp.zeros((topk, tm), F32)
        ev = jnp.zeros((topk, tm), I32)
        for i in range(topk):
            m = jnp.max(cand, axis=0, keepdims=True)
            f = jnp.min(jnp.where(cand == m, fid, topk * topk), axis=0, keepdims=True)
            sel = fid == f
            e = jnp.max(jnp.where(sel, eid, -1), axis=0, keepdims=True)
            cand = jnp.where(sel, -jnp.inf, cand)
            sv = jnp.where(slot == i, m, sv)
            ev = jnp.where(slot == i, e, ev)
        p = jnp.exp(sv - sv[0:1])
        r0 = pl.multiple_of(h * topk, topk)
        e_s[pl.ds(r0, topk), :] = ev
        g_s[pl.ds(r0, topk), :] = p * (1.0 / jnp.sum(p, axis=0, keepdims=True))
        return carry

    lax.fori_loop(0, n_heads, head, 0)
    eid_ref[...] = e_s[...].T
    gate_ref[...] = g_s[...].T


def _peer_score(x2, gf, wpqt, keys, *, tm):
    n, d = x2.shape
    n_heads, _, n_keys, _ = keys.shape
    picks = n_heads * PEER_TOPK
    fid = jnp.tile(jnp.asarray(_pair_candidates(PEER_TOPK), I32)[:, None], (1, tm))
    kern = functools.partial(_peer_score_kernel, n_heads=n_heads, n_keys=n_keys, topk=PEER_TOPK)
    row = lambda w: pl.BlockSpec((tm, w), lambda i: (i, 0))
    return pl.pallas_call(
        kern, grid=(n // tm,),
        in_specs=[row(d), _full(gf.shape), _full(wpqt.shape), _full(keys.shape), _full(fid.shape)],
        out_specs=[row(picks), row(picks)],
        out_shape=[jax.ShapeDtypeStruct((n, picks), I32), jax.ShapeDtypeStruct((n, picks), F32)],
        scratch_shapes=[pltpu.VMEM((wpqt.shape[0], tm), BF16), pltpu.VMEM((picks, tm), I32),
                        pltpu.VMEM((picks, tm), F32)],
        compiler_params=_cparams("parallel"),
    )(x2, gf, wpqt, keys, fid)


def _gelu_tanh(x):
    return 0.5 * x * (1.0 + jnp.tanh(math.sqrt(2.0 / math.pi) * (x + 0.044715 * (x * x * x))))


def _peer_gather_kernel(eid_ref, eid_next_ref, gate_ref, x2_ref, gf_ref, gfin_ref, tab_ref, y_ref, buf0, buf1,
                        sems, *, tt, picks):
    dc = x2_ref.shape[1]
    d = dc * LANES
    i = pl.program_id(0)
    bufs = (buf0, buf1)
    group = 2

    def issue(idx_ref, row, s, t):
        for k in range(picks):
            e = idx_ref[row, k]
            pltpu.make_async_copy(tab_ref.at[e], bufs[s].at[t, k // 8, :, k % 8, :], sems.at[s, t]).start(
                priority=k % DMA_THREADS)

    def wait(s, t):
        pltpu.make_async_copy(bufs[1 - s].at[t], bufs[s].at[t], sems.at[s, t]).wait()

    def compute(s, t, row):
        x2 = x2_ref[row]
        ms = jnp.sum(jnp.sum(x2 * x2, axis=1, keepdims=True), axis=0, keepdims=True) * (1.0 / d)
        hf = x2 * lax.rsqrt(ms + EPS) * gf_ref[...]
        rows = bufs[s][t]
        pu = rows[:, 0] * hf[0:1, :]
        for c in range(1, dc):
            pu = pu + rows[:, c] * hf[c:c + 1, :]
        a = jnp.sum(pu, axis=2, keepdims=True)
        act = jnp.broadcast_to(_gelu_tanh(a) * gate_ref[:, :, row:row + 1], pu.shape)
        x3 = []
        for c in range(dc):
            oc = jnp.sum(jnp.sum(rows[:, dc + c] * act, axis=0), axis=0, keepdims=True)
            x3.append(x2[c:c + 1, :] + oc)
        ms3 = x3[0] * x3[0]
        for c in range(1, dc):
            ms3 = ms3 + x3[c] * x3[c]
        scale = lax.rsqrt(jnp.sum(ms3, axis=1, keepdims=True) * (1.0 / d) + EPS)
        for c in range(dc):
            y_ref[row, c:c + 1, :] = x3[c] * scale * gfin_ref[c:c + 1, :]

    @pl.when(i == 0)
    def _():
        def token(t, carry):
            issue(eid_ref, t, 0, t)
            return carry

        lax.fori_loop(0, tt, token, 0)

    for s, idx_ref, idx_row0, out_row0 in ((0, eid_ref, tt, 0), (1, eid_next_ref, 0, tt)):
        for t0 in range(0, tt, group):
            for t in range(t0, t0 + group):
                wait(s, t)
            for t in range(t0, t0 + group):
                issue(idx_ref, idx_row0 + t, 1 - s, t)
            for t in range(t0, t0 + group):
                compute(s, t, out_row0 + t)

    @pl.when(i == pl.num_programs(0) - 1)
    def _():
        for t in range(tt):
            wait(0, t)


def _peer_gather(eid, gate, x2, gf, gfin, tab, *, tt):
    n, d = x2.shape
    picks = eid.shape[1]
    dc = d // LANES
    steps = n // (2 * tt)
    tab3 = tab.reshape(tab.shape[0], tab.shape[1] // LANES, LANES)
    gate4 = gate.reshape(steps, 2 * tt, picks // 8, 8).transpose(0, 2, 3, 1)
    kern = functools.partial(_peer_gather_kernel, tt=tt, picks=picks)
    bshape = (tt, picks // 8, 2 * dc, 8, LANES)
    y = pl.pallas_call(
        kern, grid=(steps,),
        in_specs=[pl.BlockSpec((2 * tt, picks), lambda i: (i, 0), memory_space=pltpu.SMEM),
                  pl.BlockSpec((tt, picks), lambda i: (jnp.minimum(2 * i + 2, 2 * steps - 1), 0),
                               memory_space=pltpu.SMEM),
                  pl.BlockSpec((None, picks // 8, 8, 2 * tt), lambda i: (i, 0, 0, 0)),
                  pl.BlockSpec((2 * tt, dc, LANES), lambda i: (i, 0, 0)), _full((dc, LANES)), _full((dc, LANES)),
                  pl.BlockSpec(memory_space=pl.ANY)],
        out_specs=pl.BlockSpec((2 * tt, dc, LANES), lambda i: (i, 0, 0)),
        out_shape=jax.ShapeDtypeStruct((n, dc, LANES), F32),
        scratch_shapes=[pltpu.VMEM(bshape, F32), pltpu.VMEM(bshape, F32), pltpu.SemaphoreType.DMA((2, tt))],
        compiler_params=_cparams("arbitrary"),
    )(eid, eid, gate4, x2.reshape(n, dc, LANES), gf.reshape(dc, LANES), gfin.reshape(dc, LANES), tab3)
    return y.reshape(n, d)


def _layer(x, pos_offset, k_past, v_past, h0, conv_prev, mem_k, mem_v, p, lam_init, dims, tiles):
    bsz, length, d = x.shape
    n = bsz * length
    nh, hd, ns, ng = dims["ssd_heads"], dims["ssd_head_dim"], dims["ssd_state"], dims["ssd_groups"]
    ah, qk = dims["diff_heads"], dims["qk_dim"]
    vd = 2 * qk
    x2d = x.reshape(n, d)

    tables = _rope_tables(length, pos_offset, qk)
    tm = tiles["tm"]
    if tm > length:
        tables = tuple(jnp.tile(t, (tm // length, 1)) for t in tables)
        cycle = tm
    else:
        cycle = length
    z, xbc, dt, qb, k, v, kb, vb = _in_proj(x2d, p["g_mix"], p["w_in_parts"], tables, tm, cycle, 1.0 / math.sqrt(qk))

    gw = nh * hd // ng
    h0_t = h0.reshape(bsz, ng, nh // ng, hd, ns).transpose(0, 1, 4, 2, 3).reshape(bsz, ng, ns, gw)
    cprev8 = jnp.pad(conv_prev, ((0, 0), (8 - (CONV_WIDTH - 1), 0), (0, 0)))
    q_len = min(CHUNK, length)
    y_ssd, h_last_t, conv_new = _ssd(xbc, dt, z, h0_t, cprev8, p["conv_w"], p["conv_b"], p["dt_bias_pad"],
                                     p["a_log_pad"], p["d_skip_x"], p["g_ssd"], p["expand"], bsz=bsz, length=length,
                                     q=q_len, tb=tiles["ssd_tb"], n_heads=nh, head_dim=hd, n_state=ns, n_groups=ng)
    h_last = h_last_t.reshape(bsz, ng, ns, nh // ng, hd).transpose(0, 1, 3, 4, 2).reshape(bsz, nh, hd, ns)

    lams = (p["lam_q1"], p["lam_k1"], p["lam_q2"], p["lam_k2"])
    if k_past is None:
        tq = tiles["attn_tq"]
        vt = vb.reshape(bsz, length // tq, tq, ah, vd).transpose(0, 3, 1, 4, 2)
        o = _attn_prompt(qb.reshape(bsz, length, ah * vd), kb.reshape(bsz, length, ah * vd), vt, lams,
                         p["g_subln"].reshape(vd, 1), tq=tq, n_heads=ah, qk_dim=qk, lam_init=lam_init)
    else:
        past = k_past.shape[1]
        o = _attn_sample(qb.reshape(bsz, length, ah * vd), kb.reshape(bsz, length, ah * vd),
                         vb.reshape(bsz, length, ah * vd), k_past.reshape(bsz, past, ah * vd),
                         v_past.reshape(bsz, past, ah * vd), lams, p["g_subln"].reshape(1, vd), n_heads=ah,
                         qk_dim=qk, lam_init=lam_init)

    n_mem = mem_k.shape[1]
    x2 = _mid(x2d, y_ssd, o.reshape(n, ah * vd), p["w_out_a"], p["w_out_b"], p["g_mem_q"], p["w_mq"],
              mem_k.reshape(bsz, n_mem, d), mem_v.reshape(bsz, n_mem, d), p["w_mo"], tm=tiles["mid_tm"],
              rows_per_batch=length, n_heads=dims["mem_heads"])

    eid, gate = _peer_score(x2, p["g_ffn"], p["w_pq_t"], p["peer_keys"], tm=tiles["peer_tm"])
    tt = tiles["gather_tt"]
    y = _peer_gather(eid, gate, x2, p["g_ffn"], p["g_final"], p["peer_tab"], tt=tt)
    return (y.reshape(bsz, length, d), k.reshape(bsz, length, ah, vd), v.reshape(bsz, length, ah, vd), h_last,
            conv_new)


def kernel(x_prompt, x_sample, cache_attn_k, cache_attn_v, cache_mem_k, cache_mem_v, state_ssm, state_conv, mem_prompt, g_mix, w_in, conv_w, conv_b, dt_bias, a_log, d_skip, g_ssd, lam_q1, lam_k1, lam_q2, lam_k2, g_subln, w_out, g_mem_q, g_mem_kv, w_mq, w_mk, w_mv, w_mo, g_ffn, w_pq, peer_keys, peer_u, peer_v, g_final):
    depth = w_in.shape[0]
    assert depth == 1, "the final norm is fused into the last layer's PEER kernel; one layer supported"
    bp, seq, d = x_prompt.shape
    _, _, nh, hd, ns = state_ssm.shape
    cdim = state_conv.shape[-1]
    width = nh * hd
    ng = (cdim - width) // (2 * ns)
    ah = cache_attn_k.shape[3]
    qk = lam_q1.shape[-1]
    dims = dict(ssd_heads=nh, ssd_head_dim=hd, ssd_state=ns, ssd_groups=ng, diff_heads=ah, qk_dim=qk,
                mem_heads=cache_mem_k.shape[3])
    sizes = (width, cdim, nh, ah * 2 * qk, ah * 2 * qk, ah * 2 * qk)
    splits = [sum(sizes[:i + 1]) for i in range(len(sizes) - 1)]
    n_mem = mem_prompt.shape[1]

    l = 0
    lam_init = 0.8 - 0.6 * math.exp(-0.3 * l)
    wz, wx, wdt, wq, wk, wv = jnp.split(w_in[l].astype(BF16), splits, axis=1)
    wdt = jnp.pad(wdt, ((0, 0), (0, LANES - nh)))
    pad_h = lambda a: jnp.pad(a.reshape(1, nh), ((0, 0), (0, LANES - nh)))
    expand = (jnp.arange(LANES)[:, None] == (jnp.arange(width) // hd)[None, :]).astype(F32)
    p = dict(
        g_mix=g_mix[l].reshape(1, d), w_in_parts=(wz, wx, wdt, wq, wk, wv),
        conv_w=conv_w[l], conv_b=conv_b[l].reshape(1, cdim), dt_bias_pad=pad_h(dt_bias[l]), a_log_pad=pad_h(a_log[l]),
        d_skip_x=jnp.repeat(d_skip[l], hd).reshape(1, width), g_ssd=g_ssd[l].reshape(1, width), expand=expand,
        lam_q1=lam_q1[l].reshape(1, qk), lam_k1=lam_k1[l].reshape(1, qk), lam_q2=lam_q2[l].reshape(1, qk),
        lam_k2=lam_k2[l].reshape(1, qk), g_subln=g_subln[l],
        w_out_a=w_out[l, :width].astype(BF16), w_out_b=w_out[l, width:].astype(BF16),
        g_mem_q=g_mem_q[l].reshape(1, d), w_mq=w_mq[l].astype(BF16), w_mo=w_mo[l].astype(BF16),
        g_ffn=g_ffn[l].reshape(1, d), w_pq_t=w_pq[l].T.astype(BF16), peer_keys=peer_keys[l].astype(BF16),
        peer_tab=jnp.concatenate([peer_u[l], peer_v[l]], axis=1), g_final=g_final.reshape(1, d),
    )

    mk_p, mv_p = _norm_mm2(mem_prompt.reshape(bp * n_mem, d), g_mem_kv[l].reshape(1, d), w_mk[l].astype(BF16),
                           w_mv[l].astype(BF16), tm=n_mem)
    mem_shape = (bp, n_mem) + cache_mem_k.shape[3:]
    mk_p, mv_p = mk_p.reshape(mem_shape), mv_p.reshape(mem_shape)

    tiles_p = dict(tm=min(512, seq), ssd_tb=min(512, seq), attn_tq=min(1024, seq), mid_tm=min(512, seq),
                   peer_tm=512, gather_tt=8)
    h0_p = jnp.zeros((bp, nh, hd, ns), F32)
    conv0_p = jnp.zeros((bp, CONV_WIDTH - 1, cdim), F32)
    yp, kp, vp, hp, cp = _layer(x_prompt, 0, None, None, h0_p, conv0_p, mk_p, mv_p, p, lam_init, dims, tiles_p)

    bs, dseq, _ = x_sample.shape
    past = cache_attn_k.shape[2]
    ns_rows = bs * dseq
    tiles_s = dict(tm=min(512, ns_rows), ssd_tb=dseq, mid_tm=dseq, peer_tm=min(256, ns_rows), gather_tt=8)
    ys, ksn, vsn, hsn, csn = _layer(x_sample, past, cache_attn_k[l], cache_attn_v[l], state_ssm[l], state_conv[l],
                                    cache_mem_k[l], cache_mem_v[l], p, lam_init, dims, tiles_s)

    st = lambda a: a[None]
    return (yp, ys, st(kp), st(vp), st(hp), st(cp), st(mk_p), st(mv_p), st(ksn), st(vsn), st(hsn), st(csn))
```
